```python
import jax, jax.numpy as jnp
from jax import lax
import numpy as np

D_MODEL = 1024
BATCH = 4
SEQ = 4096
DEPTH = 4

HEAD_DIM = 64
ATTN_HEADS = D_MODEL // 2 // HEAD_DIM
RWKV_HEADS = D_MODEL // 2 // HEAD_DIM
ATTN_WIDTH = ATTN_HEADS * HEAD_DIM
RWKV_WIDTH = RWKV_HEADS * HEAD_DIM
ATTN_BRANCHES = ((128, 1), (512, 4), (2048, 16))
DECAY_LORA = 64
AAA_LORA = 64
GATE_LORA = 128
RWKV_PROJ = 3 * RWKV_WIDTH + DECAY_LORA + AAA_LORA + GATE_LORA
IN_PROJ = 3 * ATTN_WIDTH + RWKV_PROJ
D_FF = 4 * D_MODEL
NORM_EPS = 1e-6
GN_EPS = HEAD_DIM * 1e-5
NEG_INF = -1e30

kernel_name = 'hybrid_dilated_attn_rwkv7_encoder'


def _rms_norm(x, g):
    xf = x.astype(jnp.float32)
    y = xf * lax.rsqrt(jnp.mean(xf * xf, axis=-1, keepdims=True) + NORM_EPS)
    return (y * g.astype(jnp.float32)).astype(x.dtype)


def _alibi_slopes(n):
    return jnp.exp2(-8.0 * jnp.arange(1, n + 1, dtype=jnp.float32) / n)


def _strided_band_attention(q, k, v, slopes, radius, dilation):
    B, S, H, HD = q.shape
    L = S // dilation
    N = B * dilation
    blk = radius
    nb = -(-L // blk)
    Lp = nb * blk

    def to_sub(t):
        return t.reshape(B, L, dilation, H, HD).transpose(0, 2, 1, 3, 4).reshape(N, L, H, HD)

    qb = jnp.pad(to_sub(q), ((0, 0), (0, Lp - L), (0, 0), (0, 0))).reshape(N, nb, blk, H, HD)

    def band(t):
        tp = jnp.pad(to_sub(t), ((0, 0), (blk, Lp - L + blk), (0, 0), (0, 0))).reshape(N, nb + 2, blk, H, HD)
        return jnp.concatenate([tp[:, :-2], tp[:, 1:-1], tp[:, 2:]], axis=2)

    kb, vb = band(k), band(v)
    s = jnp.einsum('nbqhd,nbkhd->nbhqk', qb, kb, preferred_element_type=jnp.float32)
    q_idx = jnp.arange(nb)[:, None] * blk + jnp.arange(blk)[None, :]
    k_idx = jnp.arange(nb)[:, None] * blk - blk + jnp.arange(3 * blk)[None, :]
    dist = jnp.abs(k_idx[:, None, :] - q_idx[:, :, None])
    valid = (dist <= radius) & (k_idx[:, None, :] >= 0) & (k_idx[:, None, :] < L)
    alibi = -slopes[None, :, None, None] * (dilation * dist).astype(jnp.float32)[:, None]
    s = jnp.where(valid[:, None], s + alibi, NEG_INF)
    m = jnp.max(s, axis=-1)
    p = jnp.exp(s - m[..., None])
    l = jnp.sum(p, axis=-1)
    o = jnp.einsum('nbhqk,nbkhd->nbqhd', p, vb.astype(jnp.float32)) / jnp.swapaxes(l, 2, 3)[..., None]

    def from_sub(t):
        t = t.reshape((N, Lp) + t.shape[3:])[:, :L]
        t = t.reshape((B, dilation, L) + t.shape[2:])
        return jnp.swapaxes(t, 1, 2).reshape((B, S) + t.shape[3:])

    return from_sub(o), from_sub(jnp.swapaxes(m, 2, 3)), from_sub(jnp.swapaxes(l, 2, 3))


def _dilated_window_attention(q, k, v, q_g, k_g):
    B, S, _ = q.shape
    shp = (B, S, ATTN_HEADS, HEAD_DIM)
    q = _rms_norm(q.reshape(shp), q_g) * (HEAD_DIM ** -0.5)
    k = _rms_norm(k.reshape(shp), k_g)
    v = v.reshape(shp)
    slopes = _alibi_slopes(ATTN_HEADS)
    outs, maxes, sums = [], [], []
    for window, dilation in ATTN_BRANCHES:
        o, m, l = _strided_band_attention(q, k, v, slopes, window // (2 * dilation), dilation)
        outs.append(o)
        maxes.append(m)
        sums.append(l)
    o = jnp.stack(outs)
    m = jnp.stack(maxes)
    l = jnp.stack(sums)
    wts = l * jnp.exp(m - jnp.max(m, axis=0, keepdims=True))
    wts = wts / jnp.sum(wts, axis=0, keepdims=True)
    out = jnp.sum(wts[..., None] * o, axis=0)
    return out.reshape(B, S, ATTN_WIDTH).astype(v.dtype)


def _wkv7_scan(r, w, k, v, kk, a, reverse):
    B, S, H, N = r.shape
    xs = tuple(jnp.moveaxis(t, 1, 0) for t in (r, w, k, v, kk, a))

    def step(state, inp):
        r_t, w_t, k_t, v_t, kk_t, a_t = inp
        s_kk = jnp.einsum('bhvk,bhk->bhv', state, kk_t)
        state = (state * w_t[:, :, None, :]
                 - s_kk[..., None] * (kk_t * a_t)[:, :, None, :]
                 + v_t[..., None] * k_t[:, :, None, :])
        return state, jnp.einsum('bhvk,bhk->bhv', state, r_t)

    init = jnp.zeros((B, H, N, N), jnp.float32)
    _, ys = lax.scan(step, init, xs, reverse=reverse)
    return jnp.moveaxis(ys, 0, 1)


def _rwkv7_bidirectional(u, shift_prev, shift_next, w0, w2, a0, a2, g2, k_k, k_a, r_k, gn_w, gn_b):
    B, S, _ = u.shape
    u_prev = jnp.pad(u, ((0, 0), (1, 0), (0, 0)))[:, :-1]
    u_next = jnp.pad(u, ((0, 0), (0, 1), (0, 0)))[:, 1:]
    u = (u + shift_prev * (u_prev - u) + shift_next * (u_next - u)).astype(jnp.float32)
    c1 = RWKV_WIDTH
    c4 = 3 * RWKV_WIDTH + DECAY_LORA
    r, k, v, xw, xa, xg = jnp.split(u, [c1, 2 * c1, 3 * c1, c4, c4 + AAA_LORA], axis=-1)
    shp = (B, S, RWKV_HEADS, HEAD_DIM)

    def heads(t):
        return t.reshape(shp)

    g = jax.nn.sigmoid(xg) @ g2
    kk = heads(k * k_k)
    kk = kk * lax.rsqrt(jnp.sum(kk * kk, axis=-1, keepdims=True) + 1e-12)
    tw = jnp.tanh(xw)
    ys, kts = [], []
    for direction, reverse in ((0, False), (1, True)):
        wlog = -jax.nn.softplus(-(w0[direction] + tw @ w2[direction])) - 0.5
        decay = jnp.exp(-jnp.exp(wlog))
        a = jax.nn.sigmoid(a0[direction] + xa @ a2[direction])
        kt = k * (1.0 + (a - 1.0) * k_a)
        ys.append(_wkv7_scan(heads(r), heads(decay), heads(kt), heads(v), kk, heads(a), reverse))
        kts.append(kt)
    y = ys[0] + ys[1]
    mu = jnp.mean(y, axis=-1, keepdims=True)
    var = jnp.mean(jnp.square(y - mu), axis=-1, keepdims=True)
    y = ((y - mu) * lax.rsqrt(var + GN_EPS)).reshape(B, S, RWKV_WIDTH) * gn_w + gn_b
    bonus = jnp.sum(heads(r) * heads(kts[0] + kts[1]) * r_k, axis=-1, keepdims=True) * heads(v)
    return (y + bonus.reshape(B, S, RWKV_WIDTH)) * g


def setup_inputs(seed: int = 0) -> dict:
    key = jax.random.key(seed)
    ks = jax.random.split(key, 21)
    f = jnp.float32
    L = DEPTH

    def nrm(k, shape, scale):
        return jax.random.normal(k, shape, f) * scale

    return {
        'x': nrm(ks[0], (BATCH, SEQ, D_MODEL), 1.0),
        'ln1_g': 1.0 + nrm(ks[1], (L, D_MODEL), 0.02),
        'w_in': nrm(ks[2], (L, D_MODEL, IN_PROJ), D_MODEL ** -0.5),
        'q_norm_g': 1.0 + nrm(ks[3], (L, HEAD_DIM), 0.02),
        'k_norm_g': 1.0 + nrm(ks[4], (L, HEAD_DIM), 0.02),
        'tshift_prev': jax.random.uniform(ks[5], (L, RWKV_PROJ), f, 0.0, 0.5),
        'tshift_next': jax.random.uniform(ks[6], (L, RWKV_PROJ), f, 0.0, 0.5),
        'rwkv_w0': jax.random.uniform(ks[7], (L, 2, RWKV_WIDTH), f, -4.0, 1.0),
        'rwkv_w2': nrm(ks[8], (L, 2, DECAY_LORA, RWKV_WIDTH), 0.5 * DECAY_LORA ** -0.5),
        'rwkv_a0': nrm(ks[9], (L, 2, RWKV_WIDTH), 0.1),
        'rwkv_a2': nrm(ks[10], (L, 2, AAA_LORA, RWKV_WIDTH), 0.5 * AAA_LORA ** -0.5),
        'rwkv_g2': nrm(ks[11], (L, GATE_LORA, RWKV_WIDTH), GATE_LORA ** -0.5),
        'rwkv_k_k': 0.85 + nrm(ks[12], (L, RWKV_WIDTH), 0.02),
        'rwkv_k_a': 1.0 + nrm(ks[13], (L, RWKV_WIDTH), 0.02),
        'rwkv_r_k': nrm(ks[14], (L, RWKV_HEADS, HEAD_DIM), 0.1),
        'rwkv_gn_w': 1.0 + nrm(ks[15], (L, RWKV_WIDTH), 0.02),
        'rwkv_gn_b': nrm(ks[16], (L, RWKV_WIDTH), 0.02),
        'w_out': nrm(ks[17], (L, D_MODEL, D_MODEL), 0.5 * D_MODEL ** -0.5),
        'ln2_g': 1.0 + nrm(ks[18], (L, D_MODEL), 0.02),
        'w_up': nrm(ks[19], (L, D_MODEL, D_FF), D_MODEL ** -0.5),
        'w_down': nrm(ks[20], (L, D_FF, D_MODEL), 0.5 * D_FF ** -0.5),
    }


def reference(x, ln1_g, w_in, q_norm_g, k_norm_g, tshift_prev, tshift_next, rwkv_w0, rwkv_w2,
              rwkv_a0, rwkv_a2, rwkv_g2, rwkv_k_k, rwkv_k_a, rwkv_r_k, rwkv_gn_w, rwkv_gn_b,
              w_out, ln2_g, w_up, w_down):
    for layer in range(DEPTH):
        h = _rms_norm(x, ln1_g[layer])
        proj = h @ w_in[layer]
        q = proj[..., :ATTN_WIDTH]
        k = proj[..., ATTN_WIDTH:2 * ATTN_WIDTH]
        v = proj[..., 2 * ATTN_WIDTH:3 * ATTN_WIDTH]
        u = proj[..., 3 * ATTN_WIDTH:]
        attn = _dilated_window_attention(q, k, v, q_norm_g[layer], k_norm_g[layer])
        rwkv = _rwkv7_bidirectional(u, tshift_prev[layer], tshift_next[layer], rwkv_w0[layer],
                                    rwkv_w2[layer], rwkv_a0[layer], rwkv_a2[layer], rwkv_g2[layer],
                                    rwkv_k_k[layer], rwkv_k_a[layer], rwkv_r_k[layer],
                                    rwkv_gn_w[layer], rwkv_gn_b[layer]).astype(x.dtype)
        x = x + jnp.concatenate([attn, rwkv], axis=-1) @ w_out[layer]
        h = _rms_norm(x, ln2_g[layer])
        x = x + jnp.square(jax.nn.relu(h @ w_up[layer])) @ w_down[layer]
    return x
```

```python
import functools

import numpy as np
import jax
import jax.numpy as jnp
from jax import lax
from jax.experimental import pallas as pl
from jax.experimental.pallas import tpu as pltpu

D_MODEL = 1024
HEAD_DIM = 64
N_HEADS = 8
WIDTH = N_HEADS * HEAD_DIM
ATTN_BRANCHES = ((128, 1), (512, 4), (2048, 16))
RADIUS = 64
LORA_BLOCK = 128
GATE_LORA = 128
RWKV_PROJ = 3 * WIDTH + LORA_BLOCK + GATE_LORA
D_FF = 4 * D_MODEL
NORM_EPS = 1e-6
GN_EPS = HEAD_DIM * 1e-5
NEG_INF = -1e30

LANES = 128
ROW_TILE = 512
ATTN_QB = 256
ATTN_TQ = 128
ATTN_TK = ATTN_TQ + 2 * RADIUS
PREP_ROWS = 256
CHUNK = 64
VMEM_LIMIT = 56 * 1024 * 1024

F32 = jnp.float32
BF16 = jnp.bfloat16
HIGHEST = lax.Precision.HIGHEST


def _mm(a, b, prec=None):
    return lax.dot_general(a, b, (((1,), (0,)), ((), ())), precision=prec,
                           preferred_element_type=F32)


def _mm_nt(a, b, prec=None):
    return lax.dot_general(a, b, (((1,), (1,)), ((), ())), precision=prec,
                           preferred_element_type=F32)


def _mm_tn(a, b, prec=None):
    return lax.dot_general(a, b, (((0,), (0,)), ((), ())), precision=prec,
                           preferred_element_type=F32)


def _seg_mean(x, bd):
    hi = x.astype(BF16)
    lo = (x - hi.astype(F32)).astype(BF16)
    return (_mm(hi, bd) + _mm(lo, bd)) * (1.0 / HEAD_DIM)


def _rms_rows(x, g):
    return x * lax.rsqrt(jnp.mean(x * x, axis=-1, keepdims=True) + NORM_EPS) * g


def _inproj_kernel(x_ref, g_ref, w_ref, qg_ref, kg_ref, bd_ref, q_ref, k_ref, v_ref, u_ref):
    h = _rms_rows(x_ref[...], g_ref[...]).astype(BF16)
    bd = bd_ref[...]
    q = _mm(h, w_ref[:, 0:WIDTH])
    q = q * lax.rsqrt(_seg_mean(q * q, bd) + NORM_EPS) * qg_ref[...] * (HEAD_DIM ** -0.5)
    q_ref[...] = q
    k = _mm(h, w_ref[:, WIDTH:2 * WIDTH])
    k = k * lax.rsqrt(_seg_mean(k * k, bd) + NORM_EPS) * kg_ref[...]
    k_ref[...] = k.astype(BF16)
    v_ref[...] = _mm(h, w_ref[:, 2 * WIDTH:3 * WIDTH]).astype(BF16)
    u_ref[...] = _mm(h, w_ref[:, 3 * WIDTH:])


def _in_proj(x2, ln_g, w_in, qg, kg, bd, layer):
    t = x2.shape[0]
    n_in = w_in.shape[-1]
    row = lambda i: (i, 0)
    par = lambda i: (layer, 0, 0)
    return pl.pallas_call(
        _inproj_kernel,
        grid=(t // ROW_TILE,),
        in_specs=[
            pl.BlockSpec((ROW_TILE, D_MODEL), row),
            pl.BlockSpec((None, 1, D_MODEL), par),
            pl.BlockSpec((None, D_MODEL, n_in), lambda i: (layer, 0, 0)),
            pl.BlockSpec((None, 1, WIDTH), par),
            pl.BlockSpec((None, 1, WIDTH), par),
            pl.BlockSpec((WIDTH, WIDTH), lambda i: (0, 0)),
        ],
        out_specs=[
            pl.BlockSpec((ROW_TILE, WIDTH), row),
            pl.BlockSpec((ROW_TILE, WIDTH), row),
            pl.BlockSpec((ROW_TILE, WIDTH), row),
            pl.BlockSpec((ROW_TILE, RWKV_PROJ), row),
        ],
        out_shape=[
            jax.ShapeDtypeStruct((t, WIDTH), F32),
            jax.ShapeDtypeStruct((t, WIDTH), BF16),
            jax.ShapeDtypeStruct((t, WIDTH), BF16),
            jax.ShapeDtypeStruct((t, RWKV_PROJ), F32),
        ],
        compiler_params=pltpu.CompilerParams(dimension_semantics=("parallel",),
                                             vmem_limit_bytes=VMEM_LIMIT),
        name="in_proj",
    )(x2, ln_g, w_in, qg, kg, bd)


def _attn_kernel(*refs, dilation, sub_len, first, last):
    q_ref, kp_ref, kc_ref, kn_ref, vp_ref, vc_ref, vn_ref = refs[:7]
    refs = refs[7:]
    if not first:
        acc_in_ref, ml_in_ref = refs[:2]
        refs = refs[2:]
    if last:
        (o_ref,) = refs
    else:
        acc_out_ref, ml_out_ref = refs

    blk = pl.program_id(2)
    kcat = jnp.concatenate([kp_ref[...], kc_ref[...], kn_ref[...]], axis=0)
    vcat = jnp.concatenate([vp_ref[...], vc_ref[...], vn_ref[...]], axis=0)

    rows = lax.broadcasted_iota(jnp.int32, (ATTN_TQ, ATTN_TK), 0)
    cols = lax.broadcasted_iota(jnp.int32, (ATTN_TQ, ATTN_TK), 1)
    dist = jnp.abs(cols - rows - RADIUS)
    neg_dist = dist.astype(F32) * (-float(dilation))
    lane = lax.broadcasted_iota(jnp.int32, (ATTN_TQ, LANES), 1)
    lo_half = lane < HEAD_DIM

    for a in range(0, ATTN_QB, ATTN_TQ):
        kidx = blk * ATTN_QB + (a - RADIUS) + cols
        valid = (dist <= RADIUS) & (kidx >= 0) & (kidx < sub_len)
        base = jnp.where(valid, neg_dist, NEG_INF)
        kw = kcat[a:a + ATTN_TK]
        vw = vcat[a:a + ATTN_TK]
        if not first:
            ml_prev = ml_in_ref[a:a + ATTN_TQ, :]
        m_all, l_all = [], []
        for hp in range(N_HEADS // 2):
            ls = slice(hp * LANES, (hp + 1) * LANES)
            q2 = q_ref[a:a + ATTN_TQ, ls]
            k2 = kw[:, ls]
            v2 = vw[:, ls]
            pv, alpha, lsum = [], [], []
            for e in range(2):
                h = 2 * hp + e
                slope = 2.0 ** (-8.0 * (h + 1) / N_HEADS)
                qm = jnp.where(lo_half if e == 0 else jnp.logical_not(lo_half), q2, 0.0).astype(BF16)
                s = _mm_nt(qm, k2) + slope * base
                mx = jnp.max(s, axis=-1, keepdims=True)
                if first:
                    m_new = mx
                    p = jnp.exp(s - m_new)
                    l_new = jnp.sum(p, axis=-1, keepdims=True)
                else:
                    m_prev = ml_prev[:, h:h + 1]
                    l_prev = ml_prev[:, N_HEADS + h:N_HEADS + h + 1]
                    m_new = jnp.maximum(m_prev, mx)
                    al = jnp.exp(m_prev - m_new)
                    p = jnp.exp(s - m_new)
                    l_new = al * l_prev + jnp.sum(p, axis=-1, keepdims=True)
                    alpha.append(al)
                pv.append(_mm(p.astype(BF16), v2))
                lsum.append(l_new)
                m_all.append(m_new)
                l_all.append(l_new)
            o_pair = jnp.where(lo_half, pv[0], pv[1])
            if not first:
                o_pair = jnp.where(lo_half, alpha[0], alpha[1]) * acc_in_ref[a:a + ATTN_TQ, ls] + o_pair
            if last:
                o_pair = o_pair / jnp.where(lo_half, lsum[0], lsum[1])
                o_ref[a:a + ATTN_TQ, ls] = o_pair.astype(o_ref.dtype)
            else:
                acc_out_ref[a:a + ATTN_TQ, ls] = o_pair
        if not last:
            ml = jnp.zeros((ATTN_TQ, LANES), F32)
            for h in range(N_HEADS):
                ml = jnp.where(lane == h, m_all[h], ml)
                ml = jnp.where(lane == N_HEADS + h, l_all[h], ml)
            ml_out_ref[a:a + ATTN_TQ, :] = ml


def _attn_branch(q, k, v, state, dilation, first, last):
    b, s, _ = q.shape
    sub_len = s // dilation
    nblk = sub_len // ATTN_QB
    halo_per_blk = ATTN_QB // RADIUS
    n_halo = sub_len // RADIUS
    sub = lambda t: t.reshape(b, sub_len, dilation * t.shape[-1])

    cur = lambda bi, r, i: (bi, i, r)
    prev = lambda bi, r, i: (bi, jnp.maximum(i * halo_per_blk - 1, 0), r)
    nxt = lambda bi, r, i: (bi, jnp.minimum((i + 1) * halo_per_blk, n_halo - 1), r)
    in_specs = [
        pl.BlockSpec((None, ATTN_QB, WIDTH), cur),
        pl.BlockSpec((None, RADIUS, WIDTH), prev),
        pl.BlockSpec((None, ATTN_QB, WIDTH), cur),
        pl.BlockSpec((None, RADIUS, WIDTH), nxt),
        pl.BlockSpec((None, RADIUS, WIDTH), prev),
        pl.BlockSpec((None, ATTN_QB, WIDTH), cur),
        pl.BlockSpec((None, RADIUS, WIDTH), nxt),
    ]
    args = [sub(q), sub(k), sub(k), sub(k), sub(v), sub(v), sub(v)]
    if not first:
        in_specs += [pl.BlockSpec((None, ATTN_QB, WIDTH), cur),
                     pl.BlockSpec((None, ATTN_QB, LANES), cur)]
        args += [sub(state[0]), sub(state[1])]
    if last:
        out_specs = [pl.BlockSpec((None, ATTN_QB, WIDTH), cur)]
        out_shape = [jax.ShapeDtypeStruct((b, sub_len, dilation * WIDTH), BF16)]
    else:
        out_specs = [pl.BlockSpec((None, ATTN_QB, WIDTH), cur),
                     pl.BlockSpec((None, ATTN_QB, LANES), cur)]
        out_shape = [jax.ShapeDtypeStruct((b, sub_len, dilation * WIDTH), F32),
                     jax.ShapeDtypeStruct((b, sub_len, dilation * LANES), F32)]
    outs = pl.pallas_call(
        functools.partial(_attn_kernel, dilation=dilation, sub_len=sub_len, first=first, last=last),
        grid=(b, dilation, nblk),
        in_specs=in_specs,
        out_specs=out_specs,
        out_shape=out_shape,
        compiler_params=pltpu.CompilerParams(
            dimension_semantics=("parallel", "parallel", "parallel"), vmem_limit_bytes=VMEM_LIMIT),
        name=f"attn_d{dilation}",
    )(*args)
    if last:
        return outs[0].reshape(b, s, WIDTH)
    return outs[0].reshape(b, s, WIDTH), outs[1].reshape(b, s, LANES)


def _sigmoid(x):
    return 1.0 / (1.0 + jnp.exp(-x))


def _prep_kernel(u_ref, up_ref, un_ref, sp_ref, sn_ref, w0_ref, w2_ref, a0_ref, a2_ref, g2_ref,
                 kk_ref, ka_ref, rk_ref, bd_ref,
                 r_ref, v_ref, kap_ref, lw_ref, kt_ref, bb_ref, g_ref, bonus_ref):
    i = pl.program_id(1)
    nblk = pl.num_programs(1)
    u = u_ref[...]
    row = lax.broadcasted_iota(jnp.int32, u.shape, 0)
    prev_row = jnp.where(i > 0, up_ref[7:8, :], 0.0)
    next_row = jnp.where(i < nblk - 1, un_ref[0:1, :], 0.0)
    u_prev = jnp.where(row == 0, prev_row, pltpu.roll(u, 1, 0))
    u_next = jnp.where(row == PREP_ROWS - 1, next_row, pltpu.roll(u, PREP_ROWS - 1, 0))
    um = u + sp_ref[...] * (u_prev - u) + sn_ref[...] * (u_next - u)

    bd = bd_ref[...]
    r = um[:, 0:WIDTH]
    k = um[:, WIDTH:2 * WIDTH]
    v = um[:, 2 * WIDTH:3 * WIDTH]
    lora = um[:, 3 * WIDTH:3 * WIDTH + LORA_BLOCK]
    xg = um[:, 3 * WIDTH + LORA_BLOCK:]
    r_ref[...] = r
    v_ref[...] = v
    g_ref[...] = _mm(_sigmoid(xg), g2_ref[...], HIGHEST)
    kk = k * kk_ref[...]
    kap = kk * lax.rsqrt(_seg_mean(kk * kk, bd) * HEAD_DIM + 1e-12)
    kap_ref[...] = kap
    tw = jnp.tanh(lora)
    kt_sum = jnp.zeros_like(k)
    for d in range(2):
        z = w0_ref[d:d + 1, :] + _mm(tw, w2_ref[d], HIGHEST)
        softplus = jnp.maximum(-z, 0.0) + jnp.log(1.0 + jnp.exp(-jnp.abs(z)))
        lw_ref[d] = -jnp.exp(-softplus - 0.5)
        a = _sigmoid(a0_ref[d:d + 1, :] + _mm(lora, a2_ref[d], HIGHEST))
        kt = k * (1.0 + (a - 1.0) * ka_ref[...])
        kt_ref[d] = kt
        bb_ref[d] = a * kap
        kt_sum = kt_sum + kt
    bonus_ref[...] = _seg_mean(r * kt_sum * rk_ref[...], bd) * HEAD_DIM * v


def _rwkv_prep(u, p, layer, bd):
    b, s, _ = u.shape
    nblk = s // PREP_ROWS
    halo = PREP_ROWS // 8
    cur = lambda bi, i: (bi, i, 0)
    par = lambda bi, i: (layer, 0, 0)
    par3 = par
    par4 = lambda bi, i: (layer, 0, 0, 0)
    out_cur = pl.BlockSpec((None, PREP_ROWS, WIDTH), cur)
    out_dir = pl.BlockSpec((2, None, PREP_ROWS, WIDTH), lambda bi, i: (0, bi, i, 0))
    one = jax.ShapeDtypeStruct((b, s, WIDTH), F32)
    two = jax.ShapeDtypeStruct((2, b, s, WIDTH), F32)
    return pl.pallas_call(
        _prep_kernel,
        grid=(b, nblk),
        in_specs=[
            pl.BlockSpec((None, PREP_ROWS, RWKV_PROJ), cur),
            pl.BlockSpec((None, 8, RWKV_PROJ), lambda bi, i: (bi, jnp.maximum(i * halo - 1, 0), 0)),
            pl.BlockSpec((None, 8, RWKV_PROJ),
                         lambda bi, i: (bi, jnp.minimum((i + 1) * halo, s // 8 - 1), 0)),
            pl.BlockSpec((None, 1, RWKV_PROJ), par),
            pl.BlockSpec((None, 1, RWKV_PROJ), par),
            pl.BlockSpec((None, 2, WIDTH), par3),
            pl.BlockSpec((None, 2, LORA_BLOCK, WIDTH), par4),
            pl.BlockSpec((None, 2, WIDTH), par3),
            pl.BlockSpec((None, 2, LORA_BLOCK, WIDTH), par4),
            pl.BlockSpec((None, GATE_LORA, WIDTH), par3),
            pl.BlockSpec((None, 1, WIDTH), par),
            pl.BlockSpec((None, 1, WIDTH), par),
            pl.BlockSpec((None, 1, WIDTH), par),
            pl.BlockSpec((WIDTH, WIDTH), lambda bi, i: (0, 0)),
        ],
        out_specs=[out_cur, out_cur, out_cur, out_dir, out_dir, out_dir, out_cur, out_cur],
        out_shape=[one, one, one, two, two, two, one, one],
        compiler_params=pltpu.CompilerParams(dimension_semantics=("parallel", "parallel"),
                                             vmem_limit_bytes=VMEM_LIMIT),
        name="rwkv_prep",
    )(u, u, u, p["tshift_prev"], p["tshift_next"], p["w0"], p["w2"], p["a0"], p["a2"], p["g2"],
      p["k_k"], p["k_a"], p["r_k"], bd)


def _wkv_kernel(r_ref, v_ref, kap_ref, lw_ref, kt_ref, bb_ref, y_ref, s_ref):
    d = pl.program_id(1)
    c = pl.program_id(2)

    @pl.when(c == 0)
    def _():
        s_ref[...] = jnp.zeros_like(s_ref)

    tt = lax.broadcasted_iota(jnp.int32, (CHUNK, CHUNK), 0)
    jj = lax.broadcasted_iota(jnp.int32, (CHUNK, CHUNK), 1)
    order = (tt - jj) * (1 - 2 * d)
    strict = order > 0
    incl = order >= 0
    eye = (tt == jj).astype(F32)
    same16 = (tt >> 4) == (jj >> 4)
    same32 = (tt >> 5) == (jj >> 5)
    in16 = strict & same16
    in32 = strict & same32 & jnp.logical_not(same16)
    in64 = strict & jnp.logical_not(same32)

    lw = lw_ref[...]
    cs = _mm(incl.astype(F32), lw, HIGHEST)
    tot = jnp.sum(lw, axis=0, keepdims=True)
    e_neg = jnp.exp(-cs)
    e_tail = jnp.exp(tot - cs)
    pc = jnp.exp(tot)
    rt = r_ref[...] * jnp.exp(cs)
    qt = kap_ref[...] * jnp.exp(cs - lw)
    kt = kt_ref[...]
    bb = bb_ref[...]
    kb = kt * e_neg
    bbar = bb * e_neg
    kh = kt * e_tail
    bh = bb * e_tail
    v = v_ref[...]

    mm = functools.partial(_mm, prec=HIGHEST)
    mm_nt = functools.partial(_mm_nt, prec=HIGHEST)
    mm_tn = functools.partial(_mm_tn, prec=HIGHEST)
    ys = []
    for h in range(N_HEADS):
        hs = slice(h * HEAD_DIM, (h + 1) * HEAD_DIM)
        qt_h, rt_h, v_h, bh_h = qt[:, hs], rt[:, hs], v[:, hs], bh[:, hs]
        qr = jnp.concatenate([qt_h, rt_h], axis=0)
        a_k = mm_nt(qr, kb[:, hs])
        a_b = mm_nt(qr, bbar[:, hs])
        a_qk = jnp.where(strict, a_k[:CHUNK], 0.0)
        a_rk = jnp.where(incl, a_k[CHUNK:], 0.0)
        a_qb = a_b[:CHUNK]
        a_rb = jnp.where(incl, a_b[CHUNK:], 0.0)

        n1 = jnp.where(in16, a_qb, 0.0)
        n2 = mm(n1, n1)
        n4 = mm(n2, n2)
        n8 = mm(n4, n4)
        t_inv = mm(mm(mm(eye - n1, eye + n2), eye + n4), eye + n8)
        for sel in (in32, in64):
            off = jnp.where(sel, a_qb, 0.0)
            t_inv = t_inv - mm(mm(t_inv, off), t_inv)

        av = mm(jnp.concatenate([a_qk, a_rk], axis=0), v_h)
        qh = mm(t_inv, qt_h)
        u0 = mm(t_inv, av[:CHUNK])
        rh = rt_h - mm(a_rb, qh)
        y0 = av[CHUNK:] - mm(a_rb, u0)
        m1 = mm_tn(qh, bh_h)
        vk = mm_tn(v_h, kh[:, hs]) - mm_tn(u0, bh_h)
        st = s_ref[h]
        ys.append(mm_nt(rh, st) + y0)
        s_ref[h] = st * pc[:, hs] - mm(st, m1) + vk
    y_ref[...] = jnp.concatenate(ys, axis=1)


def _wkv(r, v, kap, lw, kt, bb):
    b, s, _ = r.shape
    nc = s // CHUNK
    shared = lambda bi, d, c: (bi, c + d * (nc - 1 - 2 * c), 0)
    direc = lambda bi, d, c: (d, bi, c + d * (nc - 1 - 2 * c), 0)
    s_spec = pl.BlockSpec((None, CHUNK, WIDTH), shared)
    d_spec = pl.BlockSpec((None, None, CHUNK, WIDTH), direc)
    return pl.pallas_call(
        _wkv_kernel,
        grid=(b, 2, nc),
        in_specs=[s_spec, s_spec, s_spec, d_spec, d_spec, d_spec],
        out_specs=d_spec,
        out_shape=jax.ShapeDtypeStruct((2, b, s, WIDTH), F32),
        scratch_shapes=[pltpu.VMEM((N_HEADS, HEAD_DIM, HEAD_DIM), F32)],
        compiler_params=pltpu.CompilerParams(
            dimension_semantics=("parallel", "parallel", "arbitrary"), vmem_limit_bytes=VMEM_LIMIT),
        name="wkv7",
    )(r, v, kap, lw, kt, bb)


def _outproj_kernel(x_ref, attn_ref, y_ref, bonus_ref, g_ref, gw_ref, gb_ref, bd_ref, w_ref, o_ref):
    bd = bd_ref[...]
    y = y_ref[0] + y_ref[1]
    yc = y - _seg_mean(y, bd)
    yn = yc * lax.rsqrt(_seg_mean(yc * yc, bd) + GN_EPS) * gw_ref[...] + gb_ref[...]
    rw = ((yn + bonus_ref[...]) * g_ref[...]).astype(BF16)
    o_ref[...] = x_ref[...] + _mm(attn_ref[...], w_ref[0:WIDTH, :]) + _mm(rw, w_ref[WIDTH:, :])


def _out_proj(x2, attn2, y, bonus2, g2, gn_w, gn_b, bd, w_out, layer):
    t = x2.shape[0]
    row = lambda i: (i, 0)
    par = lambda i: (layer, 0, 0)
    return pl.pallas_call(
        _outproj_kernel,
        grid=(t // ROW_TILE,),
        in_specs=[
            pl.BlockSpec((ROW_TILE, D_MODEL), row),
            pl.BlockSpec((ROW_TILE, WIDTH), row),
            pl.BlockSpec((2, ROW_TILE, WIDTH), lambda i: (0, i, 0)),
            pl.BlockSpec((ROW_TILE, WIDTH), row),
            pl.BlockSpec((ROW_TILE, WIDTH), row),
            pl.BlockSpec((None, 1, WIDTH), par),
            pl.BlockSpec((None, 1, WIDTH), par),
            pl.BlockSpec((WIDTH, WIDTH), lambda i: (0, 0)),
            pl.BlockSpec((None, D_MODEL, D_MODEL), lambda i: (layer, 0, 0)),
        ],
        out_specs=pl.BlockSpec((ROW_TILE, D_MODEL), row),
        out_shape=jax.ShapeDtypeStruct((t, D_MODEL), F32),
        compiler_params=pltpu.CompilerParams(dimension_semantics=("parallel",),
                                             vmem_limit_bytes=VMEM_LIMIT),
        name="out_proj",
    )(x2, attn2, y, bonus2, g2, gn_w, gn_b, bd, w_out)


def _mlp_kernel(x_ref, g_ref, wu_ref, wd_ref, o_ref):
    x = x_ref[...]
    h = _rms_rows(x, g_ref[...]).astype(BF16)
    acc = x
    for c in range(0, D_FF, D_MODEL):
        a = jnp.maximum(_mm(h, wu_ref[:, c:c + D_MODEL]), 0.0)
        acc = acc + _mm((a * a).astype(BF16), wd_ref[c:c + D_MODEL, :])
    o_ref[...] = acc


def _mlp(x2, ln_g, w_up, w_down, layer):
    t = x2.shape[0]
    row = lambda i: (i, 0)
    return pl.pallas_call(
        _mlp_kernel,
        grid=(t // ROW_TILE,),
        in_specs=[
            pl.BlockSpec((ROW_TILE, D_MODEL), row),
            pl.BlockSpec((None, 1, D_MODEL), lambda i: (layer, 0, 0)),
            pl.BlockSpec((None, D_MODEL, D_FF), lambda i: (layer, 0, 0)),
            pl.BlockSpec((None, D_FF, D_MODEL), lambda i: (layer, 0, 0)),
        ],
        out_specs=pl.BlockSpec((ROW_TILE, D_MODEL), row),
        out_shape=jax.ShapeDtypeStruct((t, D_MODEL), F32),
        compiler_params=pltpu.CompilerParams(dimension_semantics=("parallel",),
                                             vmem_limit_bytes=VMEM_LIMIT),
        name="mlp",
    )(x2, ln_g, w_up, w_down)


def _pad_lora(w, lo):
    pad = [(0, 0), (0, 0), (lo, LORA_BLOCK - lo - w.shape[2]), (0, 0)]
    return jnp.pad(w, pad)


def kernel(x, ln1_g, w_in, q_norm_g, k_norm_g, tshift_prev, tshift_next, rwkv_w0, rwkv_w2, rwkv_a0,
           rwkv_a2, rwkv_g2, rwkv_k_k, rwkv_k_a, rwkv_r_k, rwkv_gn_w, rwkv_gn_b, w_out, ln2_g, w_up,
           w_down):
    b, s, dm = x.shape
    depth = w_in.shape[0]
    t = b * s
    head_of = np.arange(WIDTH) // HEAD_DIM
    bd = jnp.asarray(head_of[:, None] == head_of[None, :], BF16)

    w_in_b = w_in.astype(BF16)
    w_out_b = w_out.astype(BF16)
    w_up_b = w_up.astype(BF16)
    w_down_b = w_down.astype(BF16)
    vec = lambda p: p.reshape(depth, 1, -1)
    qg = vec(jnp.tile(q_norm_g, (1, N_HEADS)))
    kg = vec(jnp.tile(k_norm_g, (1, N_HEADS)))
    ln1_g, ln2_g, rwkv_gn_w, rwkv_gn_b = vec(ln1_g), vec(ln2_g), vec(rwkv_gn_w), vec(rwkv_gn_b)
    decay_cols = rwkv_w2.shape[2]
    prep_params = {
        "tshift_prev": vec(tshift_prev), "tshift_next": vec(tshift_next),
        "w0": rwkv_w0, "w2": _pad_lora(rwkv_w2, 0),
        "a0": rwkv_a0, "a2": _pad_lora(rwkv_a2, decay_cols),
        "g2": rwkv_g2, "k_k": vec(rwkv_k_k), "k_a": vec(rwkv_k_a), "r_k": vec(rwkv_r_k),
    }

    x2 = x.reshape(t, dm)
    for layer in range(depth):
        q, k, v, u = _in_proj(x2, ln1_g, w_in_b, qg, kg, bd, layer)
        q3, k3, v3 = (a.reshape(b, s, WIDTH) for a in (q, k, v))
        state = None
        for idx, (_, dilation) in enumerate(ATTN_BRANCHES):
            state = _attn_branch(q3, k3, v3, state, dilation, first=idx == 0,
                                 last=idx == len(ATTN_BRANCHES) - 1)
        attn = state
        r, rv, kap, lw, kt, bb, g, bonus = _rwkv_prep(u.reshape(b, s, RWKV_PROJ), prep_params,
                                                      layer, bd)
        y = _wkv(r, rv, kap, lw, kt, bb)
        x2 = _out_proj(x2, attn.reshape(t, WIDTH), y.reshape(2, t, WIDTH), bonus.reshape(t, WIDTH),
                       g.reshape(t, WIDTH), rwkv_gn_w, rwkv_gn_b, bd, w_out_b, layer)
        x2 = _mlp(x2, ln2_g, w_up_b, w_down_b, layer)
    return x2.reshape(b, s, dm)
```

```python
import functools

import numpy as np
import jax
import jax.numpy as jnp
from jax import lax
from jax.experimental import pallas as pl
from jax.experimental.pallas import tpu as pltpu

D_MODEL = 1024
HEAD_DIM = 64
N_HEADS = 8
WIDTH = N_HEADS * HEAD_DIM
ATTN_BRANCHES = ((128, 1), (512, 4), (2048, 16))
RADIUS = 64
LORA_BLOCK = 128
GATE_LORA = 128
RWKV_PROJ = 3 * WIDTH + LORA_BLOCK + GATE_LORA
D_FF = 4 * D_MODEL
NORM_EPS = 1e-6
GN_EPS = HEAD_DIM * 1e-5
NEG_INF = -1e30

LANES = 128
ROW_TILE = 512
ATTN_QB = 256
ATTN_TQ = 128
ATTN_TK = ATTN_TQ + 2 * RADIUS
PREP_ROWS = 256
CHUNK = 64
VMEM_LIMIT = 56 * 1024 * 1024

F32 = jnp.float32
BF16 = jnp.bfloat16
HIGHEST = lax.Precision.HIGHEST


def _mm(a, b, prec=None):
    return lax.dot_general(a, b, (((1,), (0,)), ((), ())), precision=prec,
                           preferred_element_type=F32)


def _mm_nt(a, b, prec=None):
    return lax.dot_general(a, b, (((1,), (1,)), ((), ())), precision=prec,
                           preferred_element_type=F32)


def _mm_tn(a, b, prec=None):
    return lax.dot_general(a, b, (((0,), (0,)), ((), ())), precision=prec,
                           preferred_element_type=F32)


def _seg_mean(x, bd):
    hi = x.astype(BF16)
    lo = (x - hi.astype(F32)).astype(BF16)
    return (_mm(hi, bd) + _mm(lo, bd)) * (1.0 / HEAD_DIM)


def _rms_rows(x, g):
    return x * lax.rsqrt(jnp.mean(x * x, axis=-1, keepdims=True) + NORM_EPS) * g


def _inproj_kernel(x_ref, g_ref, w_ref, qg_ref, kg_ref, bd_ref, q_ref, k_ref, v_ref, u_ref):
    h = _rms_rows(x_ref[...], g_ref[...]).astype(BF16)
    bd = bd_ref[...]
    q = _mm(h, w_ref[:, 0:WIDTH])
    q = q * lax.rsqrt(_seg_mean(q * q, bd) + NORM_EPS) * qg_ref[...] * (HEAD_DIM ** -0.5)
    q_ref[...] = q
    k = _mm(h, w_ref[:, WIDTH:2 * WIDTH])
    k = k * lax.rsqrt(_seg_mean(k * k, bd) + NORM_EPS) * kg_ref[...]
    k_ref[...] = k.astype(BF16)
    v_ref[...] = _mm(h, w_ref[:, 2 * WIDTH:3 * WIDTH]).astype(BF16)
    u_ref[...] = _mm(h, w_ref[:, 3 * WIDTH:])


def _in_proj(x2, ln_g, w_in, qg, kg, bd, layer):
    t = x2.shape[0]
    n_in = w_in.shape[-1]
    row = lambda i: (i, 0)
    par = lambda i: (layer, 0, 0)
    return pl.pallas_call(
        _inproj_kernel,
        grid=(t // ROW_TILE,),
        in_specs=[
            pl.BlockSpec((ROW_TILE, D_MODEL), row),
            pl.BlockSpec((None, 1, D_MODEL), par),
            pl.BlockSpec((None, D_MODEL, n_in), lambda i: (layer, 0, 0)),
            pl.BlockSpec((None, 1, WIDTH), par),
            pl.BlockSpec((None, 1, WIDTH), par),
            pl.BlockSpec((WIDTH, WIDTH), lambda i: (0, 0)),
        ],
        out_specs=[
            pl.BlockSpec((ROW_TILE, WIDTH), row),
            pl.BlockSpec((ROW_TILE, WIDTH), row),
            pl.BlockSpec((ROW_TILE, WIDTH), row),
            pl.BlockSpec((ROW_TILE, RWKV_PROJ), row),
        ],
        out_shape=[
            jax.ShapeDtypeStruct((t, WIDTH), F32),
            jax.ShapeDtypeStruct((t, WIDTH), BF16),
            jax.ShapeDtypeStruct((t, WIDTH), BF16),
            jax.ShapeDtypeStruct((t, RWKV_PROJ), F32),
        ],
        compiler_params=pltpu.CompilerParams(dimension_semantics=("parallel",),
                                             vmem_limit_bytes=VMEM_LIMIT),
        name="in_proj",
    )(x2, ln_g, w_in, qg, kg, bd)


def _attn_kernel(*refs, dilation, sub_len, first, last):
    q_ref, kp_ref, kc_ref, kn_ref, vp_ref, vc_ref, vn_ref = refs[:7]
    refs = refs[7:]
    if not first:
        acc_in_ref, ml_in_ref = refs[:2]
        refs = refs[2:]
    if last:
        (o_ref,) = refs
    else:
        acc_out_ref, ml_out_ref = refs

    blk = pl.program_id(2)
    kcat = jnp.concatenate([kp_ref[...], kc_ref[...], kn_ref[...]], axis=0)
    vcat = jnp.concatenate([vp_ref[...], vc_ref[...], vn_ref[...]], axis=0)

    rows = lax.broadcasted_iota(jnp.int32, (ATTN_TQ, ATTN_TK), 0)
    cols = lax.broadcasted_iota(jnp.int32, (ATTN_TQ, ATTN_TK), 1)
    dist = jnp.abs(cols - rows - RADIUS)
    neg_dist = dist.astype(F32) * (-float(dilation))
    lane = lax.broadcasted_iota(jnp.int32, (ATTN_TQ, LANES), 1)
    lo_half = lane < HEAD_DIM

    for a in range(0, ATTN_QB, ATTN_TQ):
        kidx = blk * ATTN_QB + (a - RADIUS) + cols
        valid = (dist <= RADIUS) & (kidx >= 0) & (kidx < sub_len)
        base = jnp.where(valid, neg_dist, NEG_INF)
        kw = kcat[a:a + ATTN_TK]
        vw = vcat[a:a + ATTN_TK]
        if not first:
            ml_prev = ml_in_ref[a:a + ATTN_TQ, :]
        m_all, l_all = [], []
        for hp in range(N_HEADS // 2):
            ls = slice(hp * LANES, (hp + 1) * LANES)
            q2 = q_ref[a:a + ATTN_TQ, ls]
            k2 = kw[:, ls]
            v2 = vw[:, ls]
            pv, alpha, lsum = [], [], []
            for e in range(2):
                h = 2 * hp + e
                slope = 2.0 ** (-8.0 * (h + 1) / N_HEADS)
                qm = jnp.where(lo_half if e == 0 else jnp.logical_not(lo_half), q2, 0.0).astype(BF16)
                s = _mm_nt(qm, k2) + slope * base
                mx = jnp.max(s, axis=-1, keepdims=True)
                if first:
                    m_new = mx
                    p = jnp.exp(s - m_new)
                    l_new = jnp.sum(p, axis=-1, keepdims=True)
                else:
                    m_prev = ml_prev[:, h:h + 1]
                    l_prev = ml_prev[:, N_HEADS + h:N_HEADS + h + 1]
                    m_new = jnp.maximum(m_prev, mx)
                    al = jnp.exp(m_prev - m_new)
                    p = jnp.exp(s - m_new)
                    l_new = al * l_prev + jnp.sum(p, axis=-1, keepdims=True)
                    alpha.append(al)
                pv.append(_mm(p.astype(BF16), v2))
                lsum.append(l_new)
                m_all.append(m_new)
                l_all.append(l_new)
            o_pair = jnp.where(lo_half, pv[0], pv[1])
            if not first:
                o_pair = jnp.where(lo_half, alpha[0], alpha[1]) * acc_in_ref[a:a + ATTN_TQ, ls] + o_pair
            if last:
                o_pair = o_pair / jnp.where(lo_half, lsum[0], lsum[1])
                o_ref[a:a + ATTN_TQ, ls] = o_pair.astype(o_ref.dtype)
            else:
                acc_out_ref[a:a + ATTN_TQ, ls] = o_pair
        if not last:
            ml = jnp.zeros((ATTN_TQ, LANES), F32)
            for h in range(N_HEADS):
                ml = jnp.where(lane == h, m_all[h], ml)
                ml = jnp.where(lane == N_HEADS + h, l_all[h], ml)
            ml_out_ref[a:a + ATTN_TQ, :] = ml


def _attn_branch(q, k, v, state, dilation, first, last):
    b, s, _ = q.shape
    sub_len = s // dilation
    nblk = sub_len // ATTN_QB
    halo_per_blk = ATTN_QB // RADIUS
    n_halo = sub_len // RADIUS
    sub = lambda t: t.reshape(b, sub_len, dilation * t.shape[-1])

    cur = lambda bi, r, i: (bi, i, r)
    prev = lambda bi, r, i: (bi, jnp.maximum(i * halo_per_blk - 1, 0), r)
    nxt = lambda bi, r, i: (bi, jnp.minimum((i + 1) * halo_per_blk, n_halo - 1), r)
    in_specs = [
        pl.BlockSpec((None, ATTN_QB, WIDTH), cur),
        pl.BlockSpec((None, RADIUS, WIDTH), prev),
        pl.BlockSpec((None, ATTN_QB, WIDTH), cur),
        pl.BlockSpec((None, RADIUS, WIDTH), nxt),
        pl.BlockSpec((None, RADIUS, WIDTH), prev),
        pl.BlockSpec((None, ATTN_QB, WIDTH), cur),
        pl.BlockSpec((None, RADIUS, WIDTH), nxt),
    ]
    args = [sub(q), sub(k), sub(k), sub(k), sub(v), sub(v), sub(v)]
    if not first:
        in_specs += [pl.BlockSpec((None, ATTN_QB, WIDTH), cur),
                     pl.BlockSpec((None, ATTN_QB, LANES), cur)]
        args += [sub(state[0]), sub(state[1])]
    if last:
        out_specs = [pl.BlockSpec((None, ATTN_QB, WIDTH), cur)]
        out_shape = [jax.ShapeDtypeStruct((b, sub_len, dilation * WIDTH), BF16)]
    else:
        out_specs = [pl.BlockSpec((None, ATTN_QB, WIDTH), cur),
                     pl.BlockSpec((None, ATTN_QB, LANES), cur)]
        out_shape = [jax.ShapeDtypeStruct((b, sub_len, dilation * WIDTH), F32),
                     jax.ShapeDtypeStruct((b, sub_len, dilation * LANES), F32)]
    outs = pl.pallas_call(
        functools.partial(_attn_kernel, dilation=dilation, sub_len=sub_len, first=first, last=last),
        grid=(b, dilation, nblk),
        in_specs=in_specs,
        out_specs=out_specs,
        out_shape=out_shape,
        compiler_params=pltpu.CompilerParams(
            dimension_semantics=("parallel", "parallel", "parallel"), vmem_limit_bytes=VMEM_LIMIT),
        name=f"attn_d{dilation}",
    )(*args)
    if last:
        return outs[0].reshape(b, s, WIDTH)
    return outs[0].reshape(b, s, WIDTH), outs[1].reshape(b, s, LANES)


def _sigmoid(x):
    return 1.0 / (1.0 + jnp.exp(-x))


def _prep_kernel(u_ref, up_ref, un_ref, sp_ref, sn_ref, w0_ref, w2_ref, a0_ref, a2_ref, g2_ref,
                 kk_ref, ka_ref, rk_ref, bd_ref,
                 r_ref, v_ref, kap_ref, lw_ref, kt_ref, bb_ref, g_ref, bonus_ref):
    i = pl.program_id(1)
    nblk = pl.num_programs(1)
    u = u_ref[...]
    row = lax.broadcasted_iota(jnp.int32, u.shape, 0)
    prev_row = jnp.where(i > 0, up_ref[7:8, :], 0.0)
    next_row = jnp.where(i < nblk - 1, un_ref[0:1, :], 0.0)
    u_prev = jnp.where(row == 0, prev_row, pltpu.roll(u, 1, 0))
    u_next = jnp.where(row == PREP_ROWS - 1, next_row, pltpu.roll(u, PREP_ROWS - 1, 0))
    um = u + sp_ref[...] * (u_prev - u) + sn_ref[...] * (u_next - u)

    bd = bd_ref[...]
    r = um[:, 0:WIDTH]
    k = um[:, WIDTH:2 * WIDTH]
    v = um[:, 2 * WIDTH:3 * WIDTH]
    lora = um[:, 3 * WIDTH:3 * WIDTH + LORA_BLOCK]
    xg = um[:, 3 * WIDTH + LORA_BLOCK:]
    r_ref[...] = r
    v_ref[...] = v
    g_ref[...] = _mm(_sigmoid(xg), g2_ref[...], HIGHEST)
    kk = k * kk_ref[...]
    kap = kk * lax.rsqrt(_seg_mean(kk * kk, bd) * HEAD_DIM + 1e-12)
    kap_ref[...] = kap
    tw = jnp.tanh(lora)
    kt_sum = jnp.zeros_like(k)
    for d in range(2):
        z = w0_ref[d:d + 1, :] + _mm(tw, w2_ref[d], HIGHEST)
        softplus = jnp.maximum(-z, 0.0) + jnp.log(1.0 + jnp.exp(-jnp.abs(z)))
        lw_ref[d] = -jnp.exp(-softplus - 0.5)
        a = _sigmoid(a0_ref[d:d + 1, :] + _mm(lora, a2_ref[d], HIGHEST))
        kt = k * (1.0 + (a - 1.0) * ka_ref[...])
        kt_ref[d] = kt
        bb_ref[d] = a * kap
        kt_sum = kt_sum + kt
    bonus_ref[...] = _seg_mean(r * kt_sum * rk_ref[...], bd) * HEAD_DIM * v


def _rwkv_prep(u, p, layer, bd):
    b, s, _ = u.shape
    nblk = s // PREP_ROWS
    halo = PREP_ROWS // 8
    cur = lambda bi, i: (bi, i, 0)
    par = lambda bi, i: (layer, 0, 0)
    par3 = par
    par4 = lambda bi, i: (layer, 0, 0, 0)
    out_cur = pl.BlockSpec((None, PREP_ROWS, WIDTH), cur)
    out_dir = pl.BlockSpec((2, None, PREP_ROWS, WIDTH), lambda bi, i: (0, bi, i, 0))
    one = jax.ShapeDtypeStruct((b, s, WIDTH), F32)
    two = jax.ShapeDtypeStruct((2, b, s, WIDTH), F32)
    return pl.pallas_call(
        _prep_kernel,
        grid=(b, nblk),
        in_specs=[
            pl.BlockSpec((None, PREP_ROWS, RWKV_PROJ), cur),
            pl.BlockSpec((None, 8, RWKV_PROJ), lambda bi, i: (bi, jnp.maximum(i * halo - 1, 0), 0)),
            pl.BlockSpec((None, 8, RWKV_PROJ),
                         lambda bi, i: (bi, jnp.minimum((i + 1) * halo, s // 8 - 1), 0)),
            pl.BlockSpec((None, 1, RWKV_PROJ), par),
            pl.BlockSpec((None, 1, RWKV_PROJ), par),
            pl.BlockSpec((None, 2, WIDTH), par3),
            pl.BlockSpec((None, 2, LORA_BLOCK, WIDTH), par4),
            pl.BlockSpec((None, 2, WIDTH), par3),
            pl.BlockSpec((None, 2, LORA_BLOCK, WIDTH), par4),
            pl.BlockSpec((None, GATE_LORA, WIDTH), par3),
            pl.BlockSpec((None, 1, WIDTH), par),
            pl.BlockSpec((None, 1, WIDTH), par),
            pl.BlockSpec((None, 1, WIDTH), par),
            pl.BlockSpec((WIDTH, WIDTH), lambda bi, i: (0, 0)),
        ],
        out_specs=[out_cur, out_cur, out_cur, out_dir, out_dir, out_dir, out_cur, out_cur],
        out_shape=[one, one, one, two, two, two, one, one],
        compiler_params=pltpu.CompilerParams(dimension_semantics=("parallel", "parallel"),
                                             vmem_limit_bytes=VMEM_LIMIT),
        name="rwkv_prep",
    )(u, u, u, p["tshift_prev"], p["tshift_next"], p["w0"], p["w2"], p["a0"], p["a2"], p["g2"],
      p["k_k"], p["k_a"], p["r_k"], bd)


def _wkv_kernel(r_ref, v_ref, kap_ref, lw_ref, kt_ref, bb_ref, y_ref, s_ref):
    d = pl.program_id(1)
    c = pl.program_id(2)

    @pl.when(c == 0)
    def _():
        s_ref[...] = jnp.zeros_like(s_ref)

    tt = lax.broadcasted_iota(jnp.int32, (CHUNK, CHUNK), 0)
    jj = lax.broadcasted_iota(jnp.int32, (CHUNK, CHUNK), 1)
    order = (tt - jj) * (1 - 2 * d)
    strict = order > 0
    incl = order >= 0
    eye = (tt == jj).astype(F32)
    same16 = (tt >> 4) == (jj >> 4)
    same32 = (tt >> 5) == (jj >> 5)
    in16 = strict & same16
    in32 = strict & same32 & jnp.logical_not(same16)
    in64 = strict & jnp.logical_not(same32)

    lw = lw_ref[...]
    cs = _mm(incl.astype(F32), lw, HIGHEST)
    tot = jnp.sum(lw, axis=0, keepdims=True)
    e_neg = jnp.exp(-cs)
    e_tail = jnp.exp(tot - cs)
    pc = jnp.exp(tot)
    rt = r_ref[...] * jnp.exp(cs)
    qt = kap_ref[...] * jnp.exp(cs - lw)
    kt = kt_ref[...]
    bb = bb_ref[...]
    kb = kt * e_neg
    bbar = bb * e_neg
    kh = kt * e_tail
    bh = bb * e_tail
    v = v_ref[...]

    b16 = lambda t: t.astype(BF16)
    mm = lambda x, y: _mm(b16(x), b16(y))
    mm_nt = lambda x, y: _mm_nt(b16(x), b16(y))
    mm_tn = lambda x, y: _mm_tn(b16(x), b16(y))
    kb, bbar, kh, bh, v = b16(kb), b16(bbar), b16(kh), b16(bh), b16(v)
    heads = range(N_HEADS)
    each = lambda f, *lists: [f(*args) for args in zip(*lists)]
    cut = lambda t: [t[:, h * HEAD_DIM:(h + 1) * HEAD_DIM] for h in heads]
    qt_h, rt_h, v_h, bh_h, kb_h, bbar_h, kh_h = map(cut, (qt, rt, v, bh, kb, bbar, kh))
    qr = each(lambda q, r: b16(jnp.concatenate([q, r], axis=0)), qt_h, rt_h)
    a_k = each(mm_nt, qr, kb_h)
    a_b = each(mm_nt, qr, bbar_h)
    a_qb = [t[:CHUNK] for t in a_b]
    a_rb = [b16(jnp.where(incl, t[CHUNK:], 0.0)) for t in a_b]

    n1 = [jnp.where(in16, t, 0.0) for t in a_qb]
    n2 = each(mm, n1, n1)
    n4 = each(mm, n2, n2)
    n8 = each(mm, n4, n4)
    t_inv = each(lambda a, c: mm(eye - a, eye + c), n1, n2)
    t_inv = each(lambda t, c: mm(t, eye + c), t_inv, n4)
    t_inv = each(lambda t, c: mm(t, eye + c), t_inv, n8)
    for sel in (in32, in64):
        t16 = [b16(t) for t in t_inv]
        tmp = each(lambda t, a: mm(t, jnp.where(sel, a, 0.0)), t16, a_qb)
        t_inv = each(lambda t, m, t6: t - mm(m, t6), t_inv, tmp, t16)

    a_kc = [jnp.concatenate([jnp.where(strict, t[:CHUNK], 0.0), jnp.where(incl, t[CHUNK:], 0.0)],
                            axis=0) for t in a_k]
    av = each(mm, a_kc, v_h)
    x = each(lambda t, q, a: b16(mm(t, jnp.concatenate([q, a[:CHUNK]], axis=1))),
             t_inv, qt_h, av)
    w = each(mm, a_rb, x)
    z = each(mm_tn, x, bh_h)
    vk = each(lambda vv, kk, zz: mm_tn(vv, kk) - zz[HEAD_DIM:], v_h, kh_h, z)
    rh = each(lambda r, ww: b16(r - ww[:, :HEAD_DIM]), rt_h, w)
    y0 = each(lambda a, ww: a[CHUNK:] - ww[:, HEAD_DIM:], av, w)
    st = [s_ref[h] for h in heads]
    st16 = [b16(t) for t in st]
    ys = each(lambda r, s6, y: mm_nt(r, s6) + y, rh, st16, y0)
    sm = each(lambda s6, zz: mm(s6, zz[:HEAD_DIM]), st16, z)
    for h in heads:
        s_ref[h] = st[h] * pc[:, h * HEAD_DIM:(h + 1) * HEAD_DIM] - sm[h] + vk[h]
    y_ref[...] = jnp.concatenate(ys, axis=1)


def _wkv(r, v, kap, lw, kt, bb):
    b, s, _ = r.shape
    nc = s // CHUNK
    shared = lambda bi, d, c: (bi, c + d * (nc - 1 - 2 * c), 0)
    direc = lambda bi, d, c: (d, bi, c + d * (nc - 1 - 2 * c), 0)
    s_spec = pl.BlockSpec((None, CHUNK, WIDTH), shared)
    d_spec = pl.BlockSpec((None, None, CHUNK, WIDTH), direc)
    return pl.pallas_call(
        _wkv_kernel,
        grid=(b, 2, nc),
        in_specs=[s_spec, s_spec, s_spec, d_spec, d_spec, d_spec],
        out_specs=d_spec,
        out_shape=jax.ShapeDtypeStruct((2, b, s, WIDTH), F32),
        scratch_shapes=[pltpu.VMEM((N_HEADS, HEAD_DIM, HEAD_DIM), F32)],
        compiler_params=pltpu.CompilerParams(
            dimension_semantics=("parallel", "parallel", "arbitrary"), vmem_limit_bytes=VMEM_LIMIT),
        name="wkv7",
    )(r, v, kap, lw, kt, bb)


def _outproj_kernel(x_ref, attn_ref, y_ref, bonus_ref, g_ref, gw_ref, gb_ref, bd_ref, w_ref, o_ref):
    bd = bd_ref[...]
    y = y_ref[0] + y_ref[1]
    yc = y - _seg_mean(y, bd)
    yn = yc * lax.rsqrt(_seg_mean(yc * yc, bd) + GN_EPS) * gw_ref[...] + gb_ref[...]
    rw = ((yn + bonus_ref[...]) * g_ref[...]).astype(BF16)
    o_ref[...] = x_ref[...] + _mm(attn_ref[...], w_ref[0:WIDTH, :]) + _mm(rw, w_ref[WIDTH:, :])


def _out_proj(x2, attn2, y, bonus2, g2, gn_w, gn_b, bd, w_out, layer):
    t = x2.shape[0]
    row = lambda i: (i, 0)
    par = lambda i: (layer, 0, 0)
    return pl.pallas_call(
        _outproj_kernel,
        grid=(t // ROW_TILE,),
        in_specs=[
            pl.BlockSpec((ROW_TILE, D_MODEL), row),
            pl.BlockSpec((ROW_TILE, WIDTH), row),
            pl.BlockSpec((2, ROW_TILE, WIDTH), lambda i: (0, i, 0)),
            pl.BlockSpec((ROW_TILE, WIDTH), row),
            pl.BlockSpec((ROW_TILE, WIDTH), row),
            pl.BlockSpec((None, 1, WIDTH), par),
            pl.BlockSpec((None, 1, WIDTH), par),
            pl.BlockSpec((WIDTH, WIDTH), lambda i: (0, 0)),
            pl.BlockSpec((None, D_MODEL, D_MODEL), lambda i: (layer, 0, 0)),
        ],
        out_specs=pl.BlockSpec((ROW_TILE, D_MODEL), row),
        out_shape=jax.ShapeDtypeStruct((t, D_MODEL), F32),
        compiler_params=pltpu.CompilerParams(dimension_semantics=("parallel",),
                                             vmem_limit_bytes=VMEM_LIMIT),
        name="out_proj",
    )(x2, attn2, y, bonus2, g2, gn_w, gn_b, bd, w_out)


def _mlp_kernel(x_ref, g_ref, wu_ref, wd_ref, o_ref):
    x = x_ref[...]
    h = _rms_rows(x, g_ref[...]).astype(BF16)
    acc = x
    for c in range(0, D_FF, D_MODEL):
        a = jnp.maximum(_mm(h, wu_ref[:, c:c + D_MODEL]), 0.0)
        acc = acc + _mm((a * a).astype(BF16), wd_ref[c:c + D_MODEL, :])
    o_ref[...] = acc


def _mlp(x2, ln_g, w_up, w_down, layer):
    t = x2.shape[0]
    row = lambda i: (i, 0)
    return pl.pallas_call(
        _mlp_kernel,
        grid=(t // ROW_TILE,),
        in_specs=[
            pl.BlockSpec((ROW_TILE, D_MODEL), row),
            pl.BlockSpec((None, 1, D_MODEL), lambda i: (layer, 0, 0)),
            pl.BlockSpec((None, D_MODEL, D_FF), lambda i: (layer, 0, 0)),
            pl.BlockSpec((None, D_FF, D_MODEL), lambda i: (layer, 0, 0)),
        ],
        out_specs=pl.BlockSpec((ROW_TILE, D_MODEL), row),
        out_shape=jax.ShapeDtypeStruct((t, D_MODEL), F32),
        compiler_params=pltpu.CompilerParams(dimension_semantics=("parallel",),
                                             vmem_limit_bytes=VMEM_LIMIT),
        name="mlp",
    )(x2, ln_g, w_up, w_down)


def _pad_lora(w, lo):
    pad = [(0, 0), (0, 0), (lo, LORA_BLOCK - lo - w.shape[2]), (0, 0)]
    return jnp.pad(w, pad)


def kernel(x, ln1_g, w_in, q_norm_g, k_norm_g, tshift_prev, tshift_next, rwkv_w0, rwkv_w2, rwkv_a0,
           rwkv_a2, rwkv_g2, rwkv_k_k, rwkv_k_a, rwkv_r_k, rwkv_gn_w, rwkv_gn_b, w_out, ln2_g, w_up,
           w_down):
    b, s, dm = x.shape
    depth = w_in.shape[0]
    t = b * s
    head_of = np.arange(WIDTH) // HEAD_DIM
    bd = jnp.asarray(head_of[:, None] == head_of[None, :], BF16)

    w_in_b = w_in.astype(BF16)
    w_out_b = w_out.astype(BF16)
    w_up_b = w_up.astype(BF16)
    w_down_b = w_down.astype(BF16)
    vec = lambda p: p.reshape(depth, 1, -1)
    qg = vec(jnp.tile(q_norm_g, (1, N_HEADS)))
    kg = vec(jnp.tile(k_norm_g, (1, N_HEADS)))
    ln1_g, ln2_g, rwkv_gn_w, rwkv_gn_b = vec(ln1_g), vec(ln2_g), vec(rwkv_gn_w), vec(rwkv_gn_b)
    decay_cols = rwkv_w2.shape[2]
    prep_params = {
        "tshift_prev": vec(tshift_prev), "tshift_next": vec(tshift_next),
        "w0": rwkv_w0, "w2": _pad_lora(rwkv_w2, 0),
        "a0": rwkv_a0, "a2": _pad_lora(rwkv_a2, decay_cols),
        "g2": rwkv_g2, "k_k": vec(rwkv_k_k), "k_a": vec(rwkv_k_a), "r_k": vec(rwkv_r_k),
    }

    x2 = x.reshape(t, dm)
    for layer in range(depth):
        q, k, v, u = _in_proj(x2, ln1_g, w_in_b, qg, kg, bd, layer)
        q3, k3, v3 = (a.reshape(b, s, WIDTH) for a in (q, k, v))
        state = None
        for idx, (_, dilation) in enumerate(ATTN_BRANCHES):
            state = _attn_branch(q3, k3, v3, state, dilation, first=idx == 0,
                                 last=idx == len(ATTN_BRANCHES) - 1)
        attn = state
        r, rv, kap, lw, kt, bb, g, bonus = _rwkv_prep(u.reshape(b, s, RWKV_PROJ), prep_params,
                                                      layer, bd)
        y = _wkv(r, rv, kap, lw, kt, bb)
        x2 = _out_proj(x2, attn.reshape(t, WIDTH), y.reshape(2, t, WIDTH), bonus.reshape(t, WIDTH),
                       g.reshape(t, WIDTH), rwkv_gn_w, rwkv_gn_b, bd, w_out_b, layer)
        x2 = _mlp(x2, ln2_g, w_up_b, w_down_b, layer)
    return x2.reshape(b, s, dm)
```

```python
import functools

import numpy as np
import jax
import jax.numpy as jnp
from jax import lax
from jax.experimental import pallas as pl
from jax.experimental.pallas import tpu as pltpu

D_MODEL = 1024
HEAD_DIM = 64
N_HEADS = 8
WIDTH = N_HEADS * HEAD_DIM
ATTN_BRANCHES = ((128, 1), (512, 4), (2048, 16))
RADIUS = 64
LORA_BLOCK = 128
GATE_LORA = 128
RWKV_PROJ = 3 * WIDTH + LORA_BLOCK + GATE_LORA
D_FF = 4 * D_MODEL
NORM_EPS = 1e-6
GN_EPS = HEAD_DIM * 1e-5
NEG_INF = -1e30

LANES = 128
ROW_TILE = 512
ATTN_QB = 256
ATTN_TQ = 128
ATTN_TK = ATTN_TQ + 2 * RADIUS
PREP_ROWS = 256
CHUNK = 64
WKV_BATCH = 2
VMEM_LIMIT = 56 * 1024 * 1024

F32 = jnp.float32
BF16 = jnp.bfloat16
HIGHEST = lax.Precision.HIGHEST


def _mm(a, b, prec=None):
    return lax.dot_general(a, b, (((1,), (0,)), ((), ())), precision=prec,
                           preferred_element_type=F32)


def _mm_nt(a, b, prec=None):
    return lax.dot_general(a, b, (((1,), (1,)), ((), ())), precision=prec,
                           preferred_element_type=F32)


def _mm_tn(a, b, prec=None):
    return lax.dot_general(a, b, (((0,), (0,)), ((), ())), precision=prec,
                           preferred_element_type=F32)


def _seg_mean(x, bd):
    hi = x.astype(BF16)
    lo = (x - hi.astype(F32)).astype(BF16)
    return (_mm(hi, bd) + _mm(lo, bd)) * (1.0 / HEAD_DIM)


def _rms_rows(x, g):
    return x * lax.rsqrt(jnp.mean(x * x, axis=-1, keepdims=True) + NORM_EPS) * g


def _inproj_kernel(x_ref, g_ref, w_ref, qg_ref, kg_ref, bd_ref, q_ref, k_ref, v_ref, u_ref):
    h = _rms_rows(x_ref[...], g_ref[...]).astype(BF16)
    bd = bd_ref[...]
    q = _mm(h, w_ref[:, 0:WIDTH])
    q = q * lax.rsqrt(_seg_mean(q * q, bd) + NORM_EPS) * qg_ref[...] * (HEAD_DIM ** -0.5)
    q_ref[...] = q
    k = _mm(h, w_ref[:, WIDTH:2 * WIDTH])
    k = k * lax.rsqrt(_seg_mean(k * k, bd) + NORM_EPS) * kg_ref[...]
    k_ref[...] = k.astype(BF16)
    v_ref[...] = _mm(h, w_ref[:, 2 * WIDTH:3 * WIDTH]).astype(BF16)
    u_ref[...] = _mm(h, w_ref[:, 3 * WIDTH:])


def _in_proj(x2, ln_g, w_in, qg, kg, bd, layer):
    t = x2.shape[0]
    n_in = w_in.shape[-1]
    row = lambda i: (i, 0)
    par = lambda i: (layer, 0, 0)
    return pl.pallas_call(
        _inproj_kernel,
        grid=(t // ROW_TILE,),
        in_specs=[
            pl.BlockSpec((ROW_TILE, D_MODEL), row),
            pl.BlockSpec((None, 1, D_MODEL), par),
            pl.BlockSpec((None, D_MODEL, n_in), lambda i: (layer, 0, 0)),
            pl.BlockSpec((None, 1, WIDTH), par),
            pl.BlockSpec((None, 1, WIDTH), par),
            pl.BlockSpec((WIDTH, WIDTH), lambda i: (0, 0)),
        ],
        out_specs=[
            pl.BlockSpec((ROW_TILE, WIDTH), row),
            pl.BlockSpec((ROW_TILE, WIDTH), row),
            pl.BlockSpec((ROW_TILE, WIDTH), row),
            pl.BlockSpec((ROW_TILE, RWKV_PROJ), row),
        ],
        out_shape=[
            jax.ShapeDtypeStruct((t, WIDTH), F32),
            jax.ShapeDtypeStruct((t, WIDTH), BF16),
            jax.ShapeDtypeStruct((t, WIDTH), BF16),
            jax.ShapeDtypeStruct((t, RWKV_PROJ), F32),
        ],
        compiler_params=pltpu.CompilerParams(dimension_semantics=("parallel",),
                                             vmem_limit_bytes=VMEM_LIMIT),
        name="in_proj",
    )(x2, ln_g, w_in, qg, kg, bd)


def _attn_kernel(*refs, dilation, sub_len, first, last):
    q_ref, kp_ref, kc_ref, kn_ref, vp_ref, vc_ref, vn_ref = refs[:7]
    refs = refs[7:]
    if not first:
        acc_in_ref, ml_in_ref = refs[:2]
        refs = refs[2:]
    if last:
        (o_ref,) = refs
    else:
        acc_out_ref, ml_out_ref = refs

    blk = pl.program_id(2)
    kcat = jnp.concatenate([kp_ref[...], kc_ref[...], kn_ref[...]], axis=0)
    vcat = jnp.concatenate([vp_ref[...], vc_ref[...], vn_ref[...]], axis=0)

    rows = lax.broadcasted_iota(jnp.int32, (ATTN_TQ, ATTN_TK), 0)
    cols = lax.broadcasted_iota(jnp.int32, (ATTN_TQ, ATTN_TK), 1)
    dist = jnp.abs(cols - rows - RADIUS)
    neg_dist = dist.astype(F32) * (-float(dilation))
    lane = lax.broadcasted_iota(jnp.int32, (ATTN_TQ, LANES), 1)
    lo_half = lane < HEAD_DIM

    for a in range(0, ATTN_QB, ATTN_TQ):
        kidx = blk * ATTN_QB + (a - RADIUS) + cols
        valid = (dist <= RADIUS) & (kidx >= 0) & (kidx < sub_len)
        base = jnp.where(valid, neg_dist, NEG_INF)
        kw = kcat[a:a + ATTN_TK]
        vw = vcat[a:a + ATTN_TK]
        if not first:
            ml_prev = ml_in_ref[a:a + ATTN_TQ, :]
        m_all, l_all = [], []
        for hp in range(N_HEADS // 2):
            ls = slice(hp * LANES, (hp + 1) * LANES)
            q2 = q_ref[a:a + ATTN_TQ, ls]
            k2 = kw[:, ls]
            v2 = vw[:, ls]
            pv, alpha, lsum = [], [], []
            for e in range(2):
                h = 2 * hp + e
                slope = 2.0 ** (-8.0 * (h + 1) / N_HEADS)
                qm = jnp.where(lo_half if e == 0 else jnp.logical_not(lo_half), q2, 0.0).astype(BF16)
                s = _mm_nt(qm, k2) + slope * base
                mx = jnp.max(s, axis=-1, keepdims=True)
                if first:
                    m_new = mx
                    p = jnp.exp(s - m_new)
                    l_new = jnp.sum(p, axis=-1, keepdims=True)
                else:
                    m_prev = ml_prev[:, h:h + 1]
                    l_prev = ml_prev[:, N_HEADS + h:N_HEADS + h + 1]
                    m_new = jnp.maximum(m_prev, mx)
                    al = jnp.exp(m_prev - m_new)
                    p = jnp.exp(s - m_new)
                    l_new = al * l_prev + jnp.sum(p, axis=-1, keepdims=True)
                    alpha.append(al)
                pv.append(_mm(p.astype(BF16), v2))
                lsum.append(l_new)
                m_all.append(m_new)
                l_all.append(l_new)
            o_pair = jnp.where(lo_half, pv[0], pv[1])
            if not first:
                o_pair = jnp.where(lo_half, alpha[0], alpha[1]) * acc_in_ref[a:a + ATTN_TQ, ls] + o_pair
            if last:
                o_pair = o_pair / jnp.where(lo_half, lsum[0], lsum[1])
                o_ref[a:a + ATTN_TQ, ls] = o_pair.astype(o_ref.dtype)
            else:
                acc_out_ref[a:a + ATTN_TQ, ls] = o_pair
        if not last:
            ml = jnp.zeros((ATTN_TQ, LANES), F32)
            for h in range(N_HEADS):
                ml = jnp.where(lane == h, m_all[h], ml)
                ml = jnp.where(lane == N_HEADS + h, l_all[h], ml)
            ml_out_ref[a:a + ATTN_TQ, :] = ml


def _attn_branch(q, k, v, state, dilation, first, last):
    b, s, _ = q.shape
    sub_len = s // dilation
    nblk = sub_len // ATTN_QB
    halo_per_blk = ATTN_QB // RADIUS
    n_halo = sub_len // RADIUS
    sub = lambda t: t.reshape(b, sub_len, dilation * t.shape[-1])

    cur = lambda bi, r, i: (bi, i, r)
    prev = lambda bi, r, i: (bi, jnp.maximum(i * halo_per_blk - 1, 0), r)
    nxt = lambda bi, r, i: (bi, jnp.minimum((i + 1) * halo_per_blk, n_halo - 1), r)
    in_specs = [
        pl.BlockSpec((None, ATTN_QB, WIDTH), cur),
        pl.BlockSpec((None, RADIUS, WIDTH), prev),
        pl.BlockSpec((None, ATTN_QB, WIDTH), cur),
        pl.BlockSpec((None, RADIUS, WIDTH), nxt),
        pl.BlockSpec((None, RADIUS, WIDTH), prev),
        pl.BlockSpec((None, ATTN_QB, WIDTH), cur),
        pl.BlockSpec((None, RADIUS, WIDTH), nxt),
    ]
    args = [sub(q), sub(k), sub(k), sub(k), sub(v), sub(v), sub(v)]
    if not first:
        in_specs += [pl.BlockSpec((None, ATTN_QB, WIDTH), cur),
                     pl.BlockSpec((None, ATTN_QB, LANES), cur)]
        args += [sub(state[0]), sub(state[1])]
    if last:
        out_specs = [pl.BlockSpec((None, ATTN_QB, WIDTH), cur)]
        out_shape = [jax.ShapeDtypeStruct((b, sub_len, dilation * WIDTH), BF16)]
    else:
        out_specs = [pl.BlockSpec((None, ATTN_QB, WIDTH), cur),
                     pl.BlockSpec((None, ATTN_QB, LANES), cur)]
        out_shape = [jax.ShapeDtypeStruct((b, sub_len, dilation * WIDTH), F32),
                     jax.ShapeDtypeStruct((b, sub_len, dilation * LANES), F32)]
    outs = pl.pallas_call(
        functools.partial(_attn_kernel, dilation=dilation, sub_len=sub_len, first=first, last=last),
        grid=(b, dilation, nblk),
        in_specs=in_specs,
        out_specs=out_specs,
        out_shape=out_shape,
        compiler_params=pltpu.CompilerParams(
            dimension_semantics=("parallel", "parallel", "parallel"), vmem_limit_bytes=VMEM_LIMIT),
        name=f"attn_d{dilation}",
    )(*args)
    if last:
        return outs[0].reshape(b, s, WIDTH)
    return outs[0].reshape(b, s, WIDTH), outs[1].reshape(b, s, LANES)


def _sigmoid(x):
    return 1.0 / (1.0 + jnp.exp(-x))


def _prep_kernel(u_ref, up_ref, un_ref, sp_ref, sn_ref, w0_ref, w2h_ref, w2l_ref, a0_ref, a2_ref,
                 g2_ref, kk_ref, ka_ref, rk_ref, bd_ref,
                 r_ref, v_ref, kap_ref, lw_ref, kt_ref, bb_ref, g_ref, bonus_ref):
    i = pl.program_id(1)
    nblk = pl.num_programs(1)
    u = u_ref[...]
    row = lax.broadcasted_iota(jnp.int32, u.shape, 0)
    prev_row = jnp.where(i > 0, up_ref[7:8, :], 0.0)
    next_row = jnp.where(i < nblk - 1, un_ref[0:1, :], 0.0)
    u_prev = jnp.where(row == 0, prev_row, pltpu.roll(u, 1, 0))
    u_next = jnp.where(row == PREP_ROWS - 1, next_row, pltpu.roll(u, PREP_ROWS - 1, 0))
    um = u + sp_ref[...] * (u_prev - u) + sn_ref[...] * (u_next - u)

    bd = bd_ref[...]
    r = um[:, 0:WIDTH]
    k = um[:, WIDTH:2 * WIDTH]
    v = um[:, 2 * WIDTH:3 * WIDTH]
    lora = um[:, 3 * WIDTH:3 * WIDTH + LORA_BLOCK]
    xg = um[:, 3 * WIDTH + LORA_BLOCK:]
    r_ref[...] = r
    v_ref[...] = v
    g_ref[...] = _mm(_sigmoid(xg).astype(BF16), g2_ref[...])
    kk = k * kk_ref[...]
    kap = kk * lax.rsqrt(_seg_mean(kk * kk, bd) * HEAD_DIM + 1e-12)
    kap_ref[...] = kap
    tw = jnp.tanh(lora)
    tw_hi = tw.astype(BF16)
    tw_lo = (tw - tw_hi.astype(F32)).astype(BF16)
    lora16 = lora.astype(BF16)
    kt_sum = jnp.zeros_like(k)
    for d in range(2):
        z = w0_ref[d:d + 1, :] + (_mm(tw_hi, w2h_ref[d])
                                  + (_mm(tw_lo, w2h_ref[d]) + _mm(tw_hi, w2l_ref[d])))
        softplus = jnp.maximum(-z, 0.0) + jnp.log(1.0 + jnp.exp(-jnp.abs(z)))
        lw_ref[d] = -jnp.exp(-softplus - 0.5)
        a = _sigmoid(a0_ref[d:d + 1, :] + _mm(lora16, a2_ref[d]))
        kt = k * (1.0 + (a - 1.0) * ka_ref[...])
        kt_ref[d] = kt
        bb_ref[d] = a * kap
        kt_sum = kt_sum + kt
    bonus_ref[...] = _seg_mean(r * kt_sum * rk_ref[...], bd) * HEAD_DIM * v


def _rwkv_prep(u, p, layer, bd):
    b, s, _ = u.shape
    nblk = s // PREP_ROWS
    halo = PREP_ROWS // 8
    cur = lambda bi, i: (bi, i, 0)
    par = lambda bi, i: (layer, 0, 0)
    par3 = par
    par4 = lambda bi, i: (layer, 0, 0, 0)
    out_cur = pl.BlockSpec((None, PREP_ROWS, WIDTH), cur)
    out_dir = pl.BlockSpec((2, None, PREP_ROWS, WIDTH), lambda bi, i: (0, bi, i, 0))
    one = jax.ShapeDtypeStruct((b, s, WIDTH), F32)
    two = jax.ShapeDtypeStruct((2, b, s, WIDTH), F32)
    return pl.pallas_call(
        _prep_kernel,
        grid=(b, nblk),
        in_specs=[
            pl.BlockSpec((None, PREP_ROWS, RWKV_PROJ), cur),
            pl.BlockSpec((None, 8, RWKV_PROJ), lambda bi, i: (bi, jnp.maximum(i * halo - 1, 0), 0)),
            pl.BlockSpec((None, 8, RWKV_PROJ),
                         lambda bi, i: (bi, jnp.minimum((i + 1) * halo, s // 8 - 1), 0)),
            pl.BlockSpec((None, 1, RWKV_PROJ), par),
            pl.BlockSpec((None, 1, RWKV_PROJ), par),
            pl.BlockSpec((None, 2, WIDTH), par3),
            pl.BlockSpec((None, 2, LORA_BLOCK, WIDTH), par4),
            pl.BlockSpec((None, 2, LORA_BLOCK, WIDTH), par4),
            pl.BlockSpec((None, 2, WIDTH), par3),
            pl.BlockSpec((None, 2, LORA_BLOCK, WIDTH), par4),
            pl.BlockSpec((None, GATE_LORA, WIDTH), par3),
            pl.BlockSpec((None, 1, WIDTH), par),
            pl.BlockSpec((None, 1, WIDTH), par),
            pl.BlockSpec((None, 1, WIDTH), par),
            pl.BlockSpec((WIDTH, WIDTH), lambda bi, i: (0, 0)),
        ],
        out_specs=[out_cur, out_cur, out_cur, out_dir, out_dir, out_dir, out_cur, out_cur],
        out_shape=[one, one, one, two, two, two, one, one],
        compiler_params=pltpu.CompilerParams(dimension_semantics=("parallel", "parallel"),
                                             vmem_limit_bytes=VMEM_LIMIT),
        name="rwkv_prep",
    )(u, u, u, p["tshift_prev"], p["tshift_next"], p["w0"], p["w2_hi"], p["w2_lo"], p["a0"], p["a2"],
      p["g2"], p["k_k"], p["k_a"], p["r_k"], bd)


def _wkv_kernel(*refs):
    in_refs = (refs[0:6], refs[6:12])
    y_refs = refs[12:14]
    s_ref = refs[14]

    @pl.when(pl.program_id(1) == 0)
    def _():
        s_ref[...] = jnp.zeros_like(s_ref)

    tt = lax.broadcasted_iota(jnp.int32, (CHUNK, LANES), 0)
    lane = lax.broadcasted_iota(jnp.int32, (CHUNK, LANES), 1)
    jj = lane & (HEAD_DIM - 1)
    first_head = lane < HEAD_DIM
    keep_a = jnp.where(first_head, 1.0, 0.0).astype(BF16)
    keep_b = jnp.where(first_head, 0.0, 1.0).astype(BF16)
    rows2 = lax.broadcasted_iota(jnp.int32, (LANES, LANES), 0) < HEAD_DIM
    cols2 = lax.broadcasted_iota(jnp.int32, (LANES, LANES), 1) < HEAD_DIM
    diag_blocks = rows2 == cols2
    eye = (tt == jj).astype(F32)
    same16 = (tt >> 4) == (jj >> 4)
    same32 = (tt >> 5) == (jj >> 5)
    b16 = lambda t: t.astype(BF16)
    bdiag = lambda t: jnp.concatenate([b16(t) * keep_a, b16(t) * keep_b], axis=0)
    mm = lambda x, y: _mm(b16(x), b16(y))
    mm_nt = lambda x, y: _mm_nt(b16(x), b16(y))
    mm_tn = lambda x, y: _mm_tn(b16(x), b16(y))
    pmm = lambda x, y: mm(x, bdiag(y))
    cut = lambda t: [t[:, p * LANES:(p + 1) * LANES] for p in range(N_HEADS // 2)]
    t_small = lax.broadcasted_iota(jnp.int32, (CHUNK, CHUNK), 0)
    j_small = lax.broadcasted_iota(jnp.int32, (CHUNK, CHUNK), 1)

    qt_h, rt_h, v_h, bh_h, kb_h, bbar_h, kh_h, pc_h = ([] for _ in range(8))
    strict, incl, in16, in32, in64 = ([] for _ in range(5))
    n_pair = N_HEADS // 2
    for d, dir_refs in enumerate(in_refs):
        before = (jj < tt) if d == 0 else (jj > tt)
        upto = jnp.logical_or(before, tt == jj)
        tri = ((j_small <= t_small) if d == 0 else (j_small >= t_small)).astype(F32)
        for bi in range(WKV_BATCH):
            r, v, kap, lw, kt, bb = (ref[bi] for ref in dir_refs)
            cs = _mm(tri, lw, HIGHEST)
            tot = jnp.sum(lw, axis=0, keepdims=True)
            e_neg = jnp.exp(-cs)
            e_tail = jnp.exp(tot - cs)
            qt_h += cut(kap * jnp.exp(cs - lw))
            rt_h += cut(r * jnp.exp(cs))
            v_h += cut(b16(v))
            kb_h += cut(b16(kt * e_neg))
            bbar_h += cut(b16(bb * e_neg))
            kh_h += cut(b16(kt * e_tail))
            bh_h += cut(b16(bb * e_tail))
            pc_h += cut(jnp.exp(tot))
            strict += [before] * n_pair
            incl += [upto] * n_pair
            in16 += [before & same16] * n_pair
            in32 += [before & same32 & jnp.logical_not(same16)] * n_pair
            in64 += [before & jnp.logical_not(same32)] * n_pair

    each = lambda f, *lists: [f(*args) for args in zip(*lists)]
    qr = each(lambda q, r: b16(jnp.concatenate([q, r], axis=0)), qt_h, rt_h)
    a_kb = each(lambda q, k, bb: mm_nt(q, jnp.concatenate([bdiag(k), bdiag(bb)], axis=0)),
                qr, kb_h, bbar_h)
    a_k = [t[:, :LANES] for t in a_kb]
    a_qb = [t[:CHUNK, LANES:] for t in a_kb]
    a_rb = each(lambda m, t: b16(jnp.where(m, t[CHUNK:, LANES:], 0.0)), incl, a_kb)

    n1 = each(lambda m, t: jnp.where(m, t, 0.0), in16, a_qb)
    n2 = each(pmm, n1, n1)
    t_inv = [eye - t for t in n1]
    pw = n2
    for _ in range(2):
        both = each(lambda p, t: pmm(jnp.concatenate([p, t], axis=0), p), pw, t_inv)
        t_inv = each(lambda t, bt: t + bt[CHUNK:], t_inv, both)
        pw = [bt[:CHUNK] for bt in both]
    t_inv = each(lambda t, p: t + pmm(t, p), t_inv, pw)
    for sel in (in32, in64):
        t16 = [b16(t) for t in t_inv]
        tmp = each(lambda t, m, a: pmm(t, jnp.where(m, a, 0.0)), t16, sel, a_qb)
        t_inv = each(lambda t, m, t6: t - pmm(m, t6), t_inv, tmp, t16)

    a_kc = each(lambda ms, mi, t: jnp.concatenate(
        [jnp.where(ms, t[:CHUNK], 0.0), jnp.where(mi, t[CHUNK:], 0.0)], axis=0), strict, incl, a_k)
    av = each(pmm, a_kc, v_h)
    x = each(lambda t, q, a: mm(t, jnp.concatenate([bdiag(q), bdiag(a[:CHUNK])], axis=1)),
             t_inv, qt_h, av)
    xq = [b16(t[:, :LANES]) for t in x]
    xu = [b16(t[:, LANES:]) for t in x]
    w = each(lambda a, q, u: mm(a, jnp.concatenate([bdiag(q), bdiag(u)], axis=1)), a_rb, xq, xu)
    rh = each(lambda r, ww: b16(r - ww[:, :LANES]), rt_h, w)
    y0 = each(lambda a, ww: a[CHUNK:] - ww[:, LANES:], av, w)
    m1 = each(lambda q, bh: b16(jnp.where(diag_blocks, mm_tn(q, bh), 0.0)), xq, bh_h)
    vk2 = each(lambda vv, u, kh, bh: mm_tn(jnp.concatenate([vv, -u], axis=0),
                                           jnp.concatenate([kh, bh], axis=0)), v_h, xu, kh_h, bh_h)
    vk = [jnp.where(first_head, t[:HEAD_DIM], t[HEAD_DIM:]) for t in vk2]
    n_chain = 2 * WKV_BATCH * n_pair
    st = [s_ref[i] for i in range(n_chain)]
    st16 = [b16(t) for t in st]
    ys = each(lambda r, s6, y: mm_nt(r, bdiag(s6)) + y, rh, st16, y0)
    sm = each(mm, st16, m1)
    for i in range(n_chain):
        s_ref[i] = st[i] * pc_h[i] - sm[i] + vk[i]
    for d, y_ref in enumerate(y_refs):
        for bi in range(WKV_BATCH):
            first = (d * WKV_BATCH + bi) * n_pair
            y_ref[bi] = jnp.concatenate(ys[first:first + n_pair], axis=1)


def _wkv(r, v, kap, lw, kt, bb):
    b, s, _ = r.shape
    nc = s // CHUNK
    specs = []
    for d in range(2):
        pos = (lambda c: c) if d == 0 else (lambda c: nc - 1 - c)
        shared = pl.BlockSpec((WKV_BATCH, CHUNK, WIDTH), lambda bi, c, pos=pos: (bi, pos(c), 0))
        direc = pl.BlockSpec((None, WKV_BATCH, CHUNK, WIDTH),
                             lambda bi, c, pos=pos, d=d: (d, bi, pos(c), 0))
        specs.append((shared, direc))
    in_specs = [sp for shared, direc in specs for sp in (shared,) * 3 + (direc,) * 3]
    y_shape = jax.ShapeDtypeStruct((b, s, WIDTH), F32)
    n_chain = 2 * WKV_BATCH * (N_HEADS // 2)
    return pl.pallas_call(
        _wkv_kernel,
        grid=(b // WKV_BATCH, nc),
        in_specs=in_specs,
        out_specs=[specs[0][0], specs[1][0]],
        out_shape=[y_shape, y_shape],
        scratch_shapes=[pltpu.VMEM((n_chain, HEAD_DIM, LANES), F32)],
        compiler_params=pltpu.CompilerParams(
            dimension_semantics=("parallel", "arbitrary"), vmem_limit_bytes=VMEM_LIMIT),
        name="wkv7",
    )(*((r, v, kap, lw, kt, bb) * 2))


def _outproj_kernel(x_ref, attn_ref, yf_ref, yr_ref, bonus_ref, g_ref, gw_ref, gb_ref, bd_ref, w_ref,
                    o_ref):
    bd = bd_ref[...]
    y = yf_ref[...] + yr_ref[...]
    yc = y - _seg_mean(y, bd)
    yn = yc * lax.rsqrt(_seg_mean(yc * yc, bd) + GN_EPS) * gw_ref[...] + gb_ref[...]
    rw = ((yn + bonus_ref[...]) * g_ref[...]).astype(BF16)
    o_ref[...] = x_ref[...] + _mm(attn_ref[...], w_ref[0:WIDTH, :]) + _mm(rw, w_ref[WIDTH:, :])


def _out_proj(x2, attn2, yf2, yr2, bonus2, g2, gn_w, gn_b, bd, w_out, layer):
    t = x2.shape[0]
    row = lambda i: (i, 0)
    par = lambda i: (layer, 0, 0)
    return pl.pallas_call(
        _outproj_kernel,
        grid=(t // ROW_TILE,),
        in_specs=[
            pl.BlockSpec((ROW_TILE, D_MODEL), row),
            pl.BlockSpec((ROW_TILE, WIDTH), row),
            pl.BlockSpec((ROW_TILE, WIDTH), row),
            pl.BlockSpec((ROW_TILE, WIDTH), row),
            pl.BlockSpec((ROW_TILE, WIDTH), row),
            pl.BlockSpec((ROW_TILE, WIDTH), row),
            pl.BlockSpec((None, 1, WIDTH), par),
            pl.BlockSpec((None, 1, WIDTH), par),
            pl.BlockSpec((WIDTH, WIDTH), lambda i: (0, 0)),
            pl.BlockSpec((None, D_MODEL, D_MODEL), lambda i: (layer, 0, 0)),
        ],
        out_specs=pl.BlockSpec((ROW_TILE, D_MODEL), row),
        out_shape=jax.ShapeDtypeStruct((t, D_MODEL), F32),
        compiler_params=pltpu.CompilerParams(dimension_semantics=("parallel",),
                                             vmem_limit_bytes=VMEM_LIMIT),
        name="out_proj",
    )(x2, attn2, yf2, yr2, bonus2, g2, gn_w, gn_b, bd, w_out)


def _mlp_kernel(x_ref, g_ref, wu_ref, wd_ref, o_ref):
    x = x_ref[...]
    h = _rms_rows(x, g_ref[...]).astype(BF16)
    acc = x
    for c in range(0, D_FF, D_MODEL):
        a = jnp.maximum(_mm(h, wu_ref[:, c:c + D_MODEL]), 0.0)
        acc = acc + _mm((a * a).astype(BF16), wd_ref[c:c + D_MODEL, :])
    o_ref[...] = acc


def _mlp(x2, ln_g, w_up, w_down, layer):
    t = x2.shape[0]
    row = lambda i: (i, 0)
    return pl.pallas_call(
        _mlp_kernel,
        grid=(t // ROW_TILE,),
        in_specs=[
            pl.BlockSpec((ROW_TILE, D_MODEL), row),
            pl.BlockSpec((None, 1, D_MODEL), lambda i: (layer, 0, 0)),
            pl.BlockSpec((None, D_MODEL, D_FF), lambda i: (layer, 0, 0)),
            pl.BlockSpec((None, D_FF, D_MODEL), lambda i: (layer, 0, 0)),
        ],
        out_specs=pl.BlockSpec((ROW_TILE, D_MODEL), row),
        out_shape=jax.ShapeDtypeStruct((t, D_MODEL), F32),
        compiler_params=pltpu.CompilerParams(dimension_semantics=("parallel",),
                                             vmem_limit_bytes=VMEM_LIMIT),
        name="mlp",
    )(x2, ln_g, w_up, w_down)


def _pad_lora(w, lo):
    pad = [(0, 0), (0, 0), (lo, LORA_BLOCK - lo - w.shape[2]), (0, 0)]
    return jnp.pad(w, pad)


def kernel(x, ln1_g, w_in, q_norm_g, k_norm_g, tshift_prev, tshift_next, rwkv_w0, rwkv_w2, rwkv_a0,
           rwkv_a2, rwkv_g2, rwkv_k_k, rwkv_k_a, rwkv_r_k, rwkv_gn_w, rwkv_gn_b, w_out, ln2_g, w_up,
           w_down):
    b, s, dm = x.shape
    depth = w_in.shape[0]
    t = b * s
    head_of = np.arange(WIDTH) // HEAD_DIM
    bd = jnp.asarray(head_of[:, None] == head_of[None, :], BF16)

    w_in_b = w_in.astype(BF16)
    w_out_b = w_out.astype(BF16)
    w_up_b = w_up.astype(BF16)
    w_down_b = w_down.astype(BF16)
    vec = lambda p: p.reshape(depth, 1, -1)
    qg = vec(jnp.tile(q_norm_g, (1, N_HEADS)))
    kg = vec(jnp.tile(k_norm_g, (1, N_HEADS)))
    ln1_g, ln2_g, rwkv_gn_w, rwkv_gn_b = vec(ln1_g), vec(ln2_g), vec(rwkv_gn_w), vec(rwkv_gn_b)
    decay_cols = rwkv_w2.shape[2]
    w2_pad = _pad_lora(rwkv_w2, 0)
    w2_hi = w2_pad.astype(BF16)
    prep_params = {
        "tshift_prev": vec(tshift_prev), "tshift_next": vec(tshift_next),
        "w0": rwkv_w0, "w2_hi": w2_hi, "w2_lo": (w2_pad - w2_hi.astype(F32)).astype(BF16),
        "a0": rwkv_a0, "a2": _pad_lora(rwkv_a2, decay_cols).astype(BF16),
        "g2": rwkv_g2.astype(BF16), "k_k": vec(rwkv_k_k), "k_a": vec(rwkv_k_a),
        "r_k": vec(rwkv_r_k),
    }

    x2 = x.reshape(t, dm)
    for layer in range(depth):
        q, k, v, u = _in_proj(x2, ln1_g, w_in_b, qg, kg, bd, layer)
        q3, k3, v3 = (a.reshape(b, s, WIDTH) for a in (q, k, v))
        state = None
        for idx, (_, dilation) in enumerate(ATTN_BRANCHES):
            state = _attn_branch(q3, k3, v3, state, dilation, first=idx == 0,
                                 last=idx == len(ATTN_BRANCHES) - 1)
        attn = state
        r, rv, kap, lw, kt, bb, g, bonus = _rwkv_prep(u.reshape(b, s, RWKV_PROJ), prep_params,
                                                      layer, bd)
        yf, yr = _wkv(r, rv, kap, lw, kt, bb)
        flat = lambda a: a.reshape(t, WIDTH)
        x2 = _out_proj(x2, flat(attn), flat(yf), flat(yr), flat(bonus), flat(g), rwkv_gn_w,
                       rwkv_gn_b, bd, w_out_b, layer)
        x2 = _mlp(x2, ln2_g, w_up_b, w_down_b, layer)
    return x2.reshape(b, s, dm)
```

```python
import functools

import numpy as np
import jax
import jax.numpy as jnp
from jax import lax
from jax.experimental import pallas as pl
from jax.experimental.pallas import tpu as pltpu

D_MODEL = 1024
HEAD_DIM = 64
N_HEADS = 8
WIDTH = N_HEADS * HEAD_DIM
ATTN_BRANCHES = ((128, 1), (512, 4), (2048, 16))
RADIUS = 64
LORA_BLOCK = 128
GATE_LORA = 128
RWKV_PROJ = 3 * WIDTH + LORA_BLOCK + GATE_LORA
D_FF = 4 * D_MODEL
NORM_EPS = 1e-6
GN_EPS = HEAD_DIM * 1e-5
NEG_INF = -1e30

LANES = 128
ROW_TILE = 512
ATTN_TQ = 128
ATTN_TK = ATTN_TQ + 2 * RADIUS
DIL_FAR = ATTN_BRANCHES[2][1]
NEAR_REACH = RADIUS * ATTN_BRANCHES[1][1]
NEAR_WIN = ATTN_TQ + 2 * NEAR_REACH
NEAR_BIAS_SHIFT = (NEAR_WIN - ATTN_TQ) // ATTN_TQ
LOG2E = 1.4426950408889634
PREP_ROWS = 256
CHUNK = 64
WKV_BATCH = 2
VMEM_LIMIT = 56 * 1024 * 1024

F32 = jnp.float32
BF16 = jnp.bfloat16
HIGHEST = lax.Precision.HIGHEST


def _mm(a, b, prec=None):
    return lax.dot_general(a, b, (((1,), (0,)), ((), ())), precision=prec,
                           preferred_element_type=F32)


def _mm_nt(a, b, prec=None):
    return lax.dot_general(a, b, (((1,), (1,)), ((), ())), precision=prec,
                           preferred_element_type=F32)


def _mm_tn(a, b, prec=None):
    return lax.dot_general(a, b, (((0,), (0,)), ((), ())), precision=prec,
                           preferred_element_type=F32)


def _seg_mean(x, bd):
    hi = x.astype(BF16)
    lo = (x - hi.astype(F32)).astype(BF16)
    return (_mm(hi, bd) + _mm(lo, bd)) * (1.0 / HEAD_DIM)


def _rms_rows(x, g):
    return x * lax.rsqrt(jnp.mean(x * x, axis=-1, keepdims=True) + NORM_EPS) * g


def _inproj_kernel(x_ref, g_ref, w_ref, qg_ref, kg_ref, bd_ref, perm_ref,
                   q_ref, k_ref, va_ref, vb_ref, q16_ref, k16_ref, va16_ref, vb16_ref, u_ref):
    h = _rms_rows(x_ref[...], g_ref[...]).astype(BF16)
    bd = bd_ref[...]
    q = _mm(h, w_ref[:, 0:WIDTH])
    q = q * lax.rsqrt(_seg_mean(q * q, bd) + NORM_EPS) * qg_ref[...] * (HEAD_DIM ** -0.5 * LOG2E)
    k = _mm(h, w_ref[:, WIDTH:2 * WIDTH])
    k = k * lax.rsqrt(_seg_mean(k * k, bd) + NORM_EPS) * kg_ref[...]
    v = _mm(h, w_ref[:, 2 * WIDTH:3 * WIDTH])
    even_head = (lax.broadcasted_iota(jnp.int32, v.shape, 1) & HEAD_DIM) == 0
    va = jnp.where(even_head, v, 1.0)
    vb = jnp.where(even_head, 1.0, v)
    perm = perm_ref[...]
    for t, nat_ref, res_ref in ((q, q_ref, q16_ref), (k, k_ref, k16_ref), (va, va_ref, va16_ref),
                                (vb, vb_ref, vb16_ref)):
        t = t.astype(BF16)
        nat_ref[...] = t
        res_ref[...] = _mm(perm, t).astype(BF16).reshape(res_ref.shape)
    u_ref[...] = _mm(h, w_ref[:, 3 * WIDTH:])


def _in_proj(x2, ln_g, w_in, qg, kg, bd, perm, layer, batch):
    t = x2.shape[0]
    n_in = w_in.shape[-1]
    seq = t // batch
    tiles = seq // ROW_TILE
    row = lambda i: (i, 0)
    par = lambda i: (layer, 0, 0)
    res_spec = pl.BlockSpec((None, DIL_FAR, ROW_TILE // DIL_FAR, WIDTH),
                            lambda i: (i // tiles, 0, i % tiles, 0))
    nat = jax.ShapeDtypeStruct((t, WIDTH), BF16)
    res = jax.ShapeDtypeStruct((batch, DIL_FAR, seq // DIL_FAR, WIDTH), BF16)
    return pl.pallas_call(
        _inproj_kernel,
        grid=(t // ROW_TILE,),
        in_specs=[
            pl.BlockSpec((ROW_TILE, D_MODEL), row),
            pl.BlockSpec((None, 1, D_MODEL), par),
            pl.BlockSpec((None, D_MODEL, n_in), lambda i: (layer, 0, 0)),
            pl.BlockSpec((None, 1, WIDTH), par),
            pl.BlockSpec((None, 1, WIDTH), par),
            pl.BlockSpec((WIDTH, WIDTH), lambda i: (0, 0)),
            pl.BlockSpec((ROW_TILE, ROW_TILE), lambda i: (0, 0)),
        ],
        out_specs=[
            pl.BlockSpec((ROW_TILE, WIDTH), row),
            pl.BlockSpec((ROW_TILE, WIDTH), row),
            pl.BlockSpec((ROW_TILE, WIDTH), row),
            pl.BlockSpec((ROW_TILE, WIDTH), row),
            res_spec, res_spec, res_spec, res_spec,
            pl.BlockSpec((ROW_TILE, RWKV_PROJ), row),
        ],
        out_shape=[nat] * 4 + [res] * 4 + [jax.ShapeDtypeStruct((t, RWKV_PROJ), F32)],
        compiler_params=pltpu.CompilerParams(dimension_semantics=("parallel",),
                                             vmem_limit_bytes=VMEM_LIMIT),
        name="in_proj",
    )(x2, ln_g, w_in, qg, kg, bd, perm)


def _slope2(h):
    return LOG2E * 2.0 ** (-8.0 * (h + 1) / N_HEADS)


def _softmax_parts(q, kw, va, vb, bias_of_head):
    tq = q.shape[0]
    lane = lax.broadcasted_iota(jnp.int32, (tq, LANES), 1)
    first_head = lane < HEAD_DIM
    heads = range(N_HEADS)
    pair = lambda t, h: t[:, (h // 2) * LANES:(h // 2 + 1) * LANES]
    own = [first_head if h % 2 == 0 else jnp.logical_not(first_head) for h in heads]
    v_own = [pair(va if h % 2 == 0 else vb, h) for h in heads]
    qm = [jnp.where(own[h], pair(q, h), 0.0).astype(BF16) for h in heads]
    s = [_mm_nt(qm[h], pair(kw, h)) + bias_of_head(h) for h in heads]
    m = [jnp.max(s[h], axis=-1, keepdims=True) for h in heads]
    p = [jnp.exp2(s[h] - m[h]).astype(BF16) for h in heads]
    pv = [_mm(p[h], v_own[h]) for h in heads]
    mx = jnp.zeros((tq, LANES), F32)
    for h in heads:
        mx = jnp.where(lane == h, m[h], mx)
    acc = [jnp.where(first_head, pv[h], pv[h + 1]) for h in heads[::2]]
    den = [pltpu.roll(jnp.where(first_head, pv[h + 1], pv[h]), HEAD_DIM, 1) for h in heads[::2]]
    return jnp.concatenate(acc, axis=1), jnp.concatenate(den, axis=1), mx


def _attn_far_kernel(q_ref, k_ref, va_ref, vb_ref, acc_ref, den_ref, mx_ref):
    sub_len = q_ref.shape[0]
    edge = lambda t: jnp.concatenate([t[:RADIUS], t, t[sub_len - RADIUS:]], axis=0)
    kcat, vacat, vbcat = edge(k_ref[...]), edge(va_ref[...]), edge(vb_ref[...])
    rows = lax.broadcasted_iota(jnp.int32, (ATTN_TQ, ATTN_TK), 0)
    cols = lax.broadcasted_iota(jnp.int32, (ATTN_TQ, ATTN_TK), 1)
    dist = jnp.abs(cols - rows - RADIUS)
    for a in range(0, sub_len, ATTN_TQ):
        kidx = a - RADIUS + cols
        valid = (dist <= RADIUS) & (kidx >= 0) & (kidx < sub_len)
        base = jnp.where(valid, dist.astype(F32) * (-float(DIL_FAR)), NEG_INF)
        win = slice(a, a + ATTN_TK)
        acc, den, mx = _softmax_parts(q_ref[a:a + ATTN_TQ, :].astype(F32), kcat[win], vacat[win],
                                      vbcat[win], lambda h: _slope2(h) * base)
        acc_ref[a:a + ATTN_TQ, :] = acc
        den_ref[a:a + ATTN_TQ, :] = den
        mx_ref[a:a + ATTN_TQ, :] = mx


def _attn_far(q16, k16, va16, vb16):
    b, n_res, sub_len, _ = q16.shape
    spec = pl.BlockSpec((None, None, sub_len, WIDTH), lambda bi, r: (bi, r, 0, 0))
    mx_spec = pl.BlockSpec((None, None, sub_len, LANES), lambda bi, r: (bi, r, 0, 0))
    wide = jax.ShapeDtypeStruct((b, n_res, sub_len, WIDTH), F32)
    return pl.pallas_call(
        _attn_far_kernel,
        grid=(b, n_res),
        in_specs=[spec, spec, spec, spec],
        out_specs=[spec, spec, mx_spec],
        out_shape=[wide, wide, jax.ShapeDtypeStruct((b, n_res, sub_len, LANES), F32)],
        compiler_params=pltpu.CompilerParams(dimension_semantics=("parallel", "parallel"),
                                             vmem_limit_bytes=VMEM_LIMIT),
        name="attn_far",
    )(q16, k16, va16, vb16)


def _attn_near_kernel(q_ref, k_ref, va_ref, vb_ref, acc_far_ref, den_far_ref, mx_far_ref, bias_ref,
                      o_ref):
    seq = k_ref.shape[0]
    t0 = pl.program_id(1) * ATTN_TQ
    w0 = pl.multiple_of(jnp.clip(t0 - NEAR_REACH, 0, seq - NEAR_WIN), ATTN_TQ)
    win = pl.ds(w0, NEAR_WIN)

    r1 = lax.broadcasted_iota(jnp.int32, (ATTN_TQ, ATTN_TQ), 0)
    c1 = lax.broadcasted_iota(jnp.int32, (ATTN_TQ, ATTN_TQ), 1)
    per = ATTN_TQ // DIL_FAR
    shift = per.bit_length() - 1
    pos_of = lambda row: DIL_FAR * (row & (per - 1)) + (row >> shift)
    to_state = jnp.where(c1 == pos_of(r1), 1.0, 0.0).astype(BF16)
    to_natural = jnp.where(r1 == pos_of(c1), 1.0, 0.0).astype(BF16)
    q = _mm(to_state, q_ref[...])

    first_blk = (w0 - t0) // ATTN_TQ + NEAR_BIAS_SHIFT
    n_blk = NEAR_WIN // ATTN_TQ
    bias = lambda h: jnp.concatenate([bias_ref[h, first_blk + c] for c in range(n_blk)], axis=1)
    acc, den, mx = _softmax_parts(q, k_ref[win, :], va_ref[win, :], vb_ref[win, :], bias)

    mx_far = mx_far_ref[...].reshape(ATTN_TQ, LANES)
    m_max = jnp.maximum(mx, mx_far)
    er = lax.broadcasted_iota(jnp.int32, (LANES, WIDTH), 0)
    ec = lax.broadcasted_iota(jnp.int32, (LANES, WIDTH), 1)
    spread = jnp.where(er == ec >> (HEAD_DIM.bit_length() - 1), 1.0, 0.0).astype(BF16)

    def per_lane(w):
        hi = w.astype(BF16)
        lo = (w - hi.astype(F32)).astype(BF16)
        return _mm(hi, spread) + _mm(lo, spread)

    w_near = per_lane(jnp.exp2(mx - m_max))
    w_far = per_lane(jnp.exp2(mx_far - m_max))
    num = w_near * acc + w_far * acc_far_ref[...].reshape(ATTN_TQ, WIDTH)
    out = num / (w_near * den + w_far * den_far_ref[...].reshape(ATTN_TQ, WIDTH))
    o_ref[...] = _mm(to_natural, out.astype(BF16)).astype(o_ref.dtype)


def _near_bias_table():
    per = ATTN_TQ // DIL_FAR
    row = np.arange(ATTN_TQ)
    pos = DIL_FAR * (row % per) + row // per
    n_cols = NEAR_WIN + NEAR_BIAS_SHIFT * ATTN_TQ
    delta = (np.arange(n_cols)[None, :] - NEAR_BIAS_SHIFT * ATTN_TQ) - pos[:, None]
    dist = np.abs(delta)
    count = (dist <= RADIUS * ATTN_BRANCHES[0][1]).astype(np.int32)
    count += (delta % ATTN_BRANCHES[1][1] == 0) & (dist <= NEAR_REACH)
    slopes = np.array([_slope2(h) for h in range(N_HEADS)], np.float64)
    with np.errstate(divide="ignore"):
        tab = -slopes[:, None, None] * dist[None] + np.log2(count)[None]
    tab = np.where(count[None] > 0, tab, NEG_INF).astype(np.float32)
    return tab.reshape(N_HEADS, ATTN_TQ, n_cols // ATTN_TQ, ATTN_TQ).transpose(0, 2, 1, 3)


def _attn_near(q, k, va, vb, far, bias):
    b, s, _ = q.shape
    per = ATTN_TQ // DIL_FAR
    whole = pl.BlockSpec((None, s, WIDTH), lambda bi, i: (bi, 0, 0))
    blk = lambda width: pl.BlockSpec((None, DIL_FAR, per, width), lambda bi, i: (bi, 0, i, 0))
    return pl.pallas_call(
        _attn_near_kernel,
        grid=(b, s // ATTN_TQ),
        in_specs=[pl.BlockSpec((None, ATTN_TQ, WIDTH), lambda bi, i: (bi, i, 0)), whole, whole, whole,
                  blk(WIDTH), blk(WIDTH), blk(LANES),
                  pl.BlockSpec(bias.shape, lambda bi, i: (0, 0, 0, 0))],
        out_specs=pl.BlockSpec((None, ATTN_TQ, WIDTH), lambda bi, i: (bi, i, 0)),
        out_shape=jax.ShapeDtypeStruct((b, s, WIDTH), BF16),
        compiler_params=pltpu.CompilerParams(dimension_semantics=("parallel", "parallel"),
                                             vmem_limit_bytes=VMEM_LIMIT),
        name="attn_near",
    )(q, k, va, vb, *far, bias)


def _sigmoid(x):
    return 1.0 / (1.0 + jnp.exp(-x))


def _prep_kernel(u_ref, up_ref, un_ref, sp_ref, sn_ref, w0_ref, w2h_ref, w2l_ref, a0_ref, a2_ref,
                 g2_ref, kk_ref, ka_ref, rk_ref, bd_ref,
                 r_ref, v_ref, kap_ref, lw_ref, kt_ref, bb_ref, g_ref, bonus_ref):
    i = pl.program_id(1)
    nblk = pl.num_programs(1)
    u = u_ref[...]
    row = lax.broadcasted_iota(jnp.int32, u.shape, 0)
    prev_row = jnp.where(i > 0, up_ref[7:8, :], 0.0)
    next_row = jnp.where(i < nblk - 1, un_ref[0:1, :], 0.0)
    u_prev = jnp.where(row == 0, prev_row, pltpu.roll(u, 1, 0))
    u_next = jnp.where(row == PREP_ROWS - 1, next_row, pltpu.roll(u, PREP_ROWS - 1, 0))
    um = u + sp_ref[...] * (u_prev - u) + sn_ref[...] * (u_next - u)

    bd = bd_ref[...]
    r = um[:, 0:WIDTH]
    k = um[:, WIDTH:2 * WIDTH]
    v = um[:, 2 * WIDTH:3 * WIDTH]
    lora = um[:, 3 * WIDTH:3 * WIDTH + LORA_BLOCK]
    xg = um[:, 3 * WIDTH + LORA_BLOCK:]
    r_ref[...] = r
    v_ref[...] = v
    g_ref[...] = _mm(_sigmoid(xg).astype(BF16), g2_ref[...])
    kk = k * kk_ref[...]
    kap = kk * lax.rsqrt(_seg_mean(kk * kk, bd) * HEAD_DIM + 1e-12)
    kap_ref[...] = kap
    tw = jnp.tanh(lora)
    tw_hi = tw.astype(BF16)
    tw_lo = (tw - tw_hi.astype(F32)).astype(BF16)
    lora16 = lora.astype(BF16)
    kt_sum = jnp.zeros_like(k)
    for d in range(2):
        z = w0_ref[d:d + 1, :] + (_mm(tw_hi, w2h_ref[d])
                                  + (_mm(tw_lo, w2h_ref[d]) + _mm(tw_hi, w2l_ref[d])))
        softplus = jnp.maximum(-z, 0.0) + jnp.log(1.0 + jnp.exp(-jnp.abs(z)))
        lw_ref[d] = -jnp.exp(-softplus - 0.5)
        a = _sigmoid(a0_ref[d:d + 1, :] + _mm(lora16, a2_ref[d]))
        kt = k * (1.0 + (a - 1.0) * ka_ref[...])
        kt_ref[d] = kt
        bb_ref[d] = a * kap
        kt_sum = kt_sum + kt
    bonus_ref[...] = _seg_mean(r * kt_sum * rk_ref[...], bd) * HEAD_DIM * v


def _rwkv_prep(u, p, layer, bd):
    b, s, _ = u.shape
    nblk = s // PREP_ROWS
    halo = PREP_ROWS // 8
    cur = lambda bi, i: (bi, i, 0)
    par = lambda bi, i: (layer, 0, 0)
    par3 = par
    par4 = lambda bi, i: (layer, 0, 0, 0)
    out_cur = pl.BlockSpec((None, PREP_ROWS, WIDTH), cur)
    out_dir = pl.BlockSpec((2, None, PREP_ROWS, WIDTH), lambda bi, i: (0, bi, i, 0))
    one = jax.ShapeDtypeStruct((b, s, WIDTH), F32)
    two = jax.ShapeDtypeStruct((2, b, s, WIDTH), F32)
    return pl.pallas_call(
        _prep_kernel,
        grid=(b, nblk),
        in_specs=[
            pl.BlockSpec((None, PREP_ROWS, RWKV_PROJ), cur),
            pl.BlockSpec((None, 8, RWKV_PROJ), lambda bi, i: (bi, jnp.maximum(i * halo - 1, 0), 0)),
            pl.BlockSpec((None, 8, RWKV_PROJ),
                         lambda bi, i: (bi, jnp.minimum((i + 1) * halo, s // 8 - 1), 0)),
            pl.BlockSpec((None, 1, RWKV_PROJ), par),
            pl.BlockSpec((None, 1, RWKV_PROJ), par),
            pl.BlockSpec((None, 2, WIDTH), par3),
            pl.BlockSpec((None, 2, LORA_BLOCK, WIDTH), par4),
            pl.BlockSpec((None, 2, LORA_BLOCK, WIDTH), par4),
            pl.BlockSpec((None, 2, WIDTH), par3),
            pl.BlockSpec((None, 2, LORA_BLOCK, WIDTH), par4),
            pl.BlockSpec((None, GATE_LORA, WIDTH), par3),
            pl.BlockSpec((None, 1, WIDTH), par),
            pl.BlockSpec((None, 1, WIDTH), par),
            pl.BlockSpec((None, 1, WIDTH), par),
            pl.BlockSpec((WIDTH, WIDTH), lambda bi, i: (0, 0)),
        ],
        out_specs=[out_cur, out_cur, out_cur, out_dir, out_dir, out_dir, out_cur, out_cur],
        out_shape=[one, one, one, two, two, two, one, one],
        compiler_params=pltpu.CompilerParams(dimension_semantics=("parallel", "parallel"),
                                             vmem_limit_bytes=VMEM_LIMIT),
        name="rwkv_prep",
    )(u, u, u, p["tshift_prev"], p["tshift_next"], p["w0"], p["w2_hi"], p["w2_lo"], p["a0"], p["a2"],
      p["g2"], p["k_k"], p["k_a"], p["r_k"], bd)


def _wkv_kernel(*refs):
    in_refs = (refs[0:6], refs[6:12])
    y_refs = refs[12:14]
    s_ref = refs[14]

    @pl.when(pl.program_id(1) == 0)
    def _():
        s_ref[...] = jnp.zeros_like(s_ref)

    tt = lax.broadcasted_iota(jnp.int32, (CHUNK, LANES), 0)
    lane = lax.broadcasted_iota(jnp.int32, (CHUNK, LANES), 1)
    jj = lane & (HEAD_DIM - 1)
    first_head = lane < HEAD_DIM
    keep_a = jnp.where(first_head, 1.0, 0.0).astype(BF16)
    keep_b = jnp.where(first_head, 0.0, 1.0).astype(BF16)
    rows2 = lax.broadcasted_iota(jnp.int32, (LANES, LANES), 0) < HEAD_DIM
    cols2 = lax.broadcasted_iota(jnp.int32, (LANES, LANES), 1) < HEAD_DIM
    diag_blocks = rows2 == cols2
    eye = (tt == jj).astype(F32)
    same16 = (tt >> 4) == (jj >> 4)
    same32 = (tt >> 5) == (jj >> 5)
    b16 = lambda t: t.astype(BF16)
    bdiag = lambda t: jnp.concatenate([b16(t) * keep_a, b16(t) * keep_b], axis=0)
    mm = lambda x, y: _mm(b16(x), b16(y))
    mm_nt = lambda x, y: _mm_nt(b16(x), b16(y))
    mm_tn = lambda x, y: _mm_tn(b16(x), b16(y))
    pmm = lambda x, y: mm(x, bdiag(y))
    cut = lambda t: [t[:, p * LANES:(p + 1) * LANES] for p in range(N_HEADS // 2)]
    t_small = lax.broadcasted_iota(jnp.int32, (CHUNK, CHUNK), 0)
    j_small = lax.broadcasted_iota(jnp.int32, (CHUNK, CHUNK), 1)

    qt_h, rt_h, v_h, bh_h, kb_h, bbar_h, kh_h, pc_h = ([] for _ in range(8))
    strict, incl, in16, in32, in64 = ([] for _ in range(5))
    n_pair = N_HEADS // 2
    for d, dir_refs in enumerate(in_refs):
        before = (jj < tt) if d == 0 else (jj > tt)
        upto = jnp.logical_or(before, tt == jj)
        tri = ((j_small <= t_small) if d == 0 else (j_small >= t_small)).astype(F32)
        for bi in range(WKV_BATCH):
            r, v, kap, lw, kt, bb = (ref[bi] for ref in dir_refs)
            cs = _mm(tri, lw, HIGHEST)
            tot = jnp.sum(lw, axis=0, keepdims=True)
            e_neg = jnp.exp(-cs)
            e_tail = jnp.exp(tot - cs)
            qt_h += cut(kap * jnp.exp(cs - lw))
            rt_h += cut(r * jnp.exp(cs))
            v_h += cut(b16(v))
            kb_h += cut(b16(kt * e_neg))
            bbar_h += cut(b16(bb * e_neg))
            kh_h += cut(b16(kt * e_tail))
            bh_h += cut(b16(bb * e_tail))
            pc_h += cut(jnp.exp(tot))
            strict += [before] * n_pair
            incl += [upto] * n_pair
            in16 += [before & same16] * n_pair
            in32 += [before & same32 & jnp.logical_not(same16)] * n_pair
            in64 += [before & jnp.logical_not(same32)] * n_pair

    each = lambda f, *lists: [f(*args) for args in zip(*lists)]
    qr = each(lambda q, r: b16(jnp.concatenate([q, r], axis=0)), qt_h, rt_h)
    a_kb = each(lambda q, k, bb: mm_nt(q, jnp.concatenate([bdiag(k), bdiag(bb)], axis=0)),
                qr, kb_h, bbar_h)
    a_k = [t[:, :LANES] for t in a_kb]
    a_qb = [t[:CHUNK, LANES:] for t in a_kb]
    a_rb = each(lambda m, t: b16(jnp.where(m, t[CHUNK:, LANES:], 0.0)), incl, a_kb)

    n1 = each(lambda m, t: jnp.where(m, t, 0.0), in16, a_qb)
    n2 = each(pmm, n1, n1)
    t_inv = [eye - t for t in n1]
    pw = n2
    for _ in range(2):
        both = each(lambda p, t: pmm(jnp.concatenate([p, t], axis=0), p), pw, t_inv)
        t_inv = each(lambda t, bt: t + bt[CHUNK:], t_inv, both)
        pw = [bt[:CHUNK] for bt in both]
    t_inv = each(lambda t, p: t + pmm(t, p), t_inv, pw)
    for sel in (in32, in64):
        t16 = [b16(t) for t in t_inv]
        tmp = each(lambda t, m, a: pmm(t, jnp.where(m, a, 0.0)), t16, sel, a_qb)
        t_inv = each(lambda t, m, t6: t - pmm(m, t6), t_inv, tmp, t16)

    a_kc = each(lambda ms, mi, t: jnp.concatenate(
        [jnp.where(ms, t[:CHUNK], 0.0), jnp.where(mi, t[CHUNK:], 0.0)], axis=0), strict, incl, a_k)
    av = each(pmm, a_kc, v_h)
    x = each(lambda t, q, a: mm(t, jnp.concatenate([bdiag(q), bdiag(a[:CHUNK])], axis=1)),
             t_inv, qt_h, av)
    xq = [b16(t[:, :LANES]) for t in x]
    xu = [b16(t[:, LANES:]) for t in x]
    w = each(lambda a, q, u: mm(a, jnp.concatenate([bdiag(q), bdiag(u)], axis=1)), a_rb, xq, xu)
    rh = each(lambda r, ww: b16(r - ww[:, :LANES]), rt_h, w)
    y0 = each(lambda a, ww: a[CHUNK:] - ww[:, LANES:], av, w)
    m1 = each(lambda q, bh: b16(jnp.where(diag_blocks, mm_tn(q, bh), 0.0)), xq, bh_h)
    vk2 = each(lambda vv, u, kh, bh: mm_tn(jnp.concatenate([vv, -u], axis=0),
                                           jnp.concatenate([kh, bh], axis=0)), v_h, xu, kh_h, bh_h)
    vk = [jnp.where(first_head, t[:HEAD_DIM], t[HEAD_DIM:]) for t in vk2]
    n_chain = 2 * WKV_BATCH * n_pair
    st = [s_ref[i] for i in range(n_chain)]
    st16 = [b16(t) for t in st]
    ys = each(lambda r, s6, y: mm_nt(r, bdiag(s6)) + y, rh, st16, y0)
    sm = each(mm, st16, m1)
    for i in range(n_chain):
        s_ref[i] = st[i] * pc_h[i] - sm[i] + vk[i]
    for d, y_ref in enumerate(y_refs):
        for bi in range(WKV_BATCH):
            first = (d * WKV_BATCH + bi) * n_pair
            y_ref[bi] = jnp.concatenate(ys[first:first + n_pair], axis=1)


def _wkv(r, v, kap, lw, kt, bb):
    b, s, _ = r.shape
    nc = s // CHUNK
    specs = []
    for d in range(2):
        pos = (lambda c: c) if d == 0 else (lambda c: nc - 1 - c)
        shared = pl.BlockSpec((WKV_BATCH, CHUNK, WIDTH), lambda bi, c, pos=pos: (bi, pos(c), 0))
        direc = pl.BlockSpec((None, WKV_BATCH, CHUNK, WIDTH),
                             lambda bi, c, pos=pos, d=d: (d, bi, pos(c), 0))
        specs.append((shared, direc))
    in_specs = [sp for shared, direc in specs for sp in (shared,) * 3 + (direc,) * 3]
    y_shape = jax.ShapeDtypeStruct((b, s, WIDTH), F32)
    n_chain = 2 * WKV_BATCH * (N_HEADS // 2)
    return pl.pallas_call(
        _wkv_kernel,
        grid=(b // WKV_BATCH, nc),
        in_specs=in_specs,
        out_specs=[specs[0][0], specs[1][0]],
        out_shape=[y_shape, y_shape],
        scratch_shapes=[pltpu.VMEM((n_chain, HEAD_DIM, LANES), F32)],
        compiler_params=pltpu.CompilerParams(
            dimension_semantics=("parallel", "arbitrary"), vmem_limit_bytes=VMEM_LIMIT),
        name="wkv7",
    )(*((r, v, kap, lw, kt, bb) * 2))


def _outproj_kernel(x_ref, attn_ref, yf_ref, yr_ref, bonus_ref, g_ref, gw_ref, gb_ref, bd_ref, w_ref,
                    o_ref):
    bd = bd_ref[...]
    y = yf_ref[...] + yr_ref[...]
    yc = y - _seg_mean(y, bd)
    yn = yc * lax.rsqrt(_seg_mean(yc * yc, bd) + GN_EPS) * gw_ref[...] + gb_ref[...]
    rw = ((yn + bonus_ref[...]) * g_ref[...]).astype(BF16)
    o_ref[...] = x_ref[...] + _mm(attn_ref[...], w_ref[0:WIDTH, :]) + _mm(rw, w_ref[WIDTH:, :])


def _out_proj(x2, attn2, yf2, yr2, bonus2, g2, gn_w, gn_b, bd, w_out, layer):
    t = x2.shape[0]
    row = lambda i: (i, 0)
    par = lambda i: (layer, 0, 0)
    return pl.pallas_call(
        _outproj_kernel,
        grid=(t // ROW_TILE,),
        in_specs=[
            pl.BlockSpec((ROW_TILE, D_MODEL), row),
            pl.BlockSpec((ROW_TILE, WIDTH), row),
            pl.BlockSpec((ROW_TILE, WIDTH), row),
            pl.BlockSpec((ROW_TILE, WIDTH), row),
            pl.BlockSpec((ROW_TILE, WIDTH), row),
            pl.BlockSpec((ROW_TILE, WIDTH), row),
            pl.BlockSpec((None, 1, WIDTH), par),
            pl.BlockSpec((None, 1, WIDTH), par),
            pl.BlockSpec((WIDTH, WIDTH), lambda i: (0, 0)),
            pl.BlockSpec((None, D_MODEL, D_MODEL), lambda i: (layer, 0, 0)),
        ],
        out_specs=pl.BlockSpec((ROW_TILE, D_MODEL), row),
        out_shape=jax.ShapeDtypeStruct((t, D_MODEL), F32),
        compiler_params=pltpu.CompilerParams(dimension_semantics=("parallel",),
                                             vmem_limit_bytes=VMEM_LIMIT),
        name="out_proj",
    )(x2, attn2, yf2, yr2, bonus2, g2, gn_w, gn_b, bd, w_out)


def _mlp_kernel(x_ref, g_ref, wu_ref, wd_ref, o_ref):
    x = x_ref[...]
    h = _rms_rows(x, g_ref[...]).astype(BF16)
    acc = x
    for c in range(0, D_FF, D_MODEL):
        a = jnp.maximum(_mm(h, wu_ref[:, c:c + D_MODEL]), 0.0)
        acc = acc + _mm((a * a).astype(BF16), wd_ref[c:c + D_MODEL, :])
    o_ref[...] = acc


def _mlp(x2, ln_g, w_up, w_down, layer):
    t = x2.shape[0]
    row = lambda i: (i, 0)
    return pl.pallas_call(
        _mlp_kernel,
        grid=(t // ROW_TILE,),
        in_specs=[
            pl.BlockSpec((ROW_TILE, D_MODEL), row),
            pl.BlockSpec((None, 1, D_MODEL), lambda i: (layer, 0, 0)),
            pl.BlockSpec((None, D_MODEL, D_FF), lambda i: (layer, 0, 0)),
            pl.BlockSpec((None, D_FF, D_MODEL), lambda i: (layer, 0, 0)),
        ],
        out_specs=pl.BlockSpec((ROW_TILE, D_MODEL), row),
        out_shape=jax.ShapeDtypeStruct((t, D_MODEL), F32),
        compiler_params=pltpu.CompilerParams(dimension_semantics=("parallel",),
                                             vmem_limit_bytes=VMEM_LIMIT),
        name="mlp",
    )(x2, ln_g, w_up, w_down)


def _pad_lora(w, lo):
    pad = [(0, 0), (0, 0), (lo, LORA_BLOCK - lo - w.shape[2]), (0, 0)]
    return jnp.pad(w, pad)


def kernel(x, ln1_g, w_in, q_norm_g, k_norm_g, tshift_prev, tshift_next, rwkv_w0, rwkv_w2, rwkv_a0,
           rwkv_a2, rwkv_g2, rwkv_k_k, rwkv_k_a, rwkv_r_k, rwkv_gn_w, rwkv_gn_b, w_out, ln2_g, w_up,
           w_down):
    b, s, dm = x.shape
    depth = w_in.shape[0]
    t = b * s
    head_of = np.arange(WIDTH) // HEAD_DIM
    bd = jnp.asarray(head_of[:, None] == head_of[None, :], BF16)
    out_row = np.arange(ROW_TILE)
    src_row = DIL_FAR * (out_row % (ROW_TILE // DIL_FAR)) + out_row // (ROW_TILE // DIL_FAR)
    perm = jnp.asarray(src_row[:, None] == np.arange(ROW_TILE)[None, :], BF16)
    near_bias = jnp.asarray(_near_bias_table())

    w_in_b = w_in.astype(BF16)
    w_out_b = w_out.astype(BF16)
    w_up_b = w_up.astype(BF16)
    w_down_b = w_down.astype(BF16)
    vec = lambda p: p.reshape(depth, 1, -1)
    qg = vec(jnp.tile(q_norm_g, (1, N_HEADS)))
    kg = vec(jnp.tile(k_norm_g, (1, N_HEADS)))
    ln1_g, ln2_g, rwkv_gn_w, rwkv_gn_b = vec(ln1_g), vec(ln2_g), vec(rwkv_gn_w), vec(rwkv_gn_b)
    decay_cols = rwkv_w2.shape[2]
    w2_pad = _pad_lora(rwkv_w2, 0)
    w2_hi = w2_pad.astype(BF16)
    prep_params = {
        "tshift_prev": vec(tshift_prev), "tshift_next": vec(tshift_next),
        "w0": rwkv_w0, "w2_hi": w2_hi, "w2_lo": (w2_pad - w2_hi.astype(F32)).astype(BF16),
        "a0": rwkv_a0, "a2": _pad_lora(rwkv_a2, decay_cols).astype(BF16),
        "g2": rwkv_g2.astype(BF16), "k_k": vec(rwkv_k_k), "k_a": vec(rwkv_k_a),
        "r_k": vec(rwkv_r_k),
    }

    x2 = x.reshape(t, dm)
    for layer in range(depth):
        q, k, va, vb, q16, k16, va16, vb16, u = _in_proj(x2, ln1_g, w_in_b, qg, kg, bd, perm, layer, b)
        far = _attn_far(q16, k16, va16, vb16)
        attn = _attn_near(*(a.reshape(b, s, WIDTH) for a in (q, k, va, vb)), far, near_bias)
        r, rv, kap, lw, kt, bb, g, bonus = _rwkv_prep(u.reshape(b, s, RWKV_PROJ), prep_params,
                                                      layer, bd)
        yf, yr = _wkv(r, rv, kap, lw, kt, bb)
        flat = lambda a: a.reshape(t, WIDTH)
        x2 = _out_proj(x2, flat(attn), flat(yf), flat(yr), flat(bonus), flat(g), rwkv_gn_w,
                       rwkv_gn_b, bd, w_out_b, layer)
        x2 = _mlp(x2, ln2_g, w_up_b, w_down_b, layer)
    return x2.reshape(b, s, dm)
```

```python
import functools

import numpy as np
import jax
import jax.numpy as jnp
from jax import lax
from jax.experimental import pallas as pl
from jax.experimental.pallas import tpu as pltpu

D_MODEL = 1024
HEAD_DIM = 64
N_HEADS = 8
WIDTH = N_HEADS * HEAD_DIM
ATTN_BRANCHES = ((128, 1), (512, 4), (2048, 16))
RADIUS = 64
LORA_BLOCK = 128
GATE_LORA = 128
RWKV_PROJ = 3 * WIDTH + LORA_BLOCK + GATE_LORA
D_FF = 4 * D_MODEL
NORM_EPS = 1e-6
GN_EPS = HEAD_DIM * 1e-5
NEG_INF = -1e30

LANES = 128
ROW_TILE = 512
PERM_ROWS = 256
ATTN_TQ = 128
ATTN_TK = ATTN_TQ + 2 * RADIUS
DIL_FAR = ATTN_BRANCHES[2][1]
NEAR_REACH = RADIUS * ATTN_BRANCHES[1][1]
NEAR_WIN = ATTN_TQ + 2 * NEAR_REACH
NEAR_BIAS_SHIFT = (NEAR_WIN - ATTN_TQ) // ATTN_TQ
LOG2E = 1.4426950408889634
PREP_ROWS = 256
CHUNK = 64
WKV_BATCH = 2
VMEM_LIMIT = 56 * 1024 * 1024

F32 = jnp.float32
BF16 = jnp.bfloat16


def _mm(a, b, prec=None):
    return lax.dot_general(a, b, (((1,), (0,)), ((), ())), precision=prec,
                           preferred_element_type=F32)


def _mm_nt(a, b, prec=None):
    return lax.dot_general(a, b, (((1,), (1,)), ((), ())), precision=prec,
                           preferred_element_type=F32)


def _mm_tn(a, b, prec=None):
    return lax.dot_general(a, b, (((0,), (0,)), ((), ())), precision=prec,
                           preferred_element_type=F32)


def _seg_mean(x, bd, split=True):
    hi = x.astype(BF16)
    total = _mm(hi, bd)
    if split:
        total = total + _mm((x - hi.astype(F32)).astype(BF16), bd)
    return total * (1.0 / HEAD_DIM)


def _rms_rows(x, g):
    return x * lax.rsqrt(jnp.mean(x * x, axis=-1, keepdims=True) + NORM_EPS) * g


def _inproj_kernel(x_ref, g_ref, w_ref, qg_ref, kg_ref, bd_ref, perm_ref,
                   q_ref, k_ref, va_ref, vb_ref, q16_ref, k16_ref, va16_ref, vb16_ref, u_ref):
    h = _rms_rows(x_ref[...], g_ref[...]).astype(BF16)
    bd = bd_ref[...]
    q = _mm(h, w_ref[:, 0:WIDTH])
    q = q * (lax.rsqrt(_seg_mean(q * q, bd, split=False) + NORM_EPS) * qg_ref[...]
             * (HEAD_DIM ** -0.5 * LOG2E))
    k = _mm(h, w_ref[:, WIDTH:2 * WIDTH])
    k = k * lax.rsqrt(_seg_mean(k * k, bd, split=False) + NORM_EPS) * kg_ref[...]
    v = _mm(h, w_ref[:, 2 * WIDTH:3 * WIDTH])
    even_head = (lax.broadcasted_iota(jnp.int32, v.shape, 1) & HEAD_DIM) == 0
    va = jnp.where(even_head, v, 1.0)
    vb = jnp.where(even_head, 1.0, v)
    perm = perm_ref[...]
    for t, nat_ref, res_ref in ((q, q_ref, q16_ref), (k, k_ref, k16_ref), (va, va_ref, va16_ref),
                                (vb, vb_ref, vb16_ref)):
        t = t.astype(BF16)
        nat_ref[...] = t
        per = PERM_ROWS // DIL_FAR
        for part in range(ROW_TILE // PERM_ROWS):
            grouped = _mm(perm, t[part * PERM_ROWS:(part + 1) * PERM_ROWS]).astype(BF16)
            res_ref[:, part * per:(part + 1) * per, :] = grouped.reshape(DIL_FAR, per, WIDTH)
    u_ref[...] = _mm(h, w_ref[:, 3 * WIDTH:])


def _in_proj(x2, ln_g, w_in, qg, kg, bd, perm, layer, batch):
    t = x2.shape[0]
    n_in = w_in.shape[-1]
    seq = t // batch
    tiles = seq // ROW_TILE
    row = lambda i: (i, 0)
    par = lambda i: (layer, 0, 0)
    res_spec = pl.BlockSpec((None, DIL_FAR, ROW_TILE // DIL_FAR, WIDTH),
                            lambda i: (i // tiles, 0, i % tiles, 0))
    nat = jax.ShapeDtypeStruct((t, WIDTH), BF16)
    res = jax.ShapeDtypeStruct((batch, DIL_FAR, seq // DIL_FAR, WIDTH), BF16)
    return pl.pallas_call(
        _inproj_kernel,
        grid=(t // ROW_TILE,),
        in_specs=[
            pl.BlockSpec((ROW_TILE, D_MODEL), row),
            pl.BlockSpec((None, 1, D_MODEL), par),
            pl.BlockSpec((None, D_MODEL, n_in), lambda i: (layer, 0, 0)),
            pl.BlockSpec((None, 1, WIDTH), par),
            pl.BlockSpec((None, 1, WIDTH), par),
            pl.BlockSpec((WIDTH, WIDTH), lambda i: (0, 0)),
            pl.BlockSpec((PERM_ROWS, PERM_ROWS), lambda i: (0, 0)),
        ],
        out_specs=[
            pl.BlockSpec((ROW_TILE, WIDTH), row),
            pl.BlockSpec((ROW_TILE, WIDTH), row),
            pl.BlockSpec((ROW_TILE, WIDTH), row),
            pl.BlockSpec((ROW_TILE, WIDTH), row),
            res_spec, res_spec, res_spec, res_spec,
            pl.BlockSpec((ROW_TILE, RWKV_PROJ), row),
        ],
        out_shape=[nat] * 4 + [res] * 4 + [jax.ShapeDtypeStruct((t, RWKV_PROJ), F32)],
        compiler_params=pltpu.CompilerParams(dimension_semantics=("parallel",),
                                             vmem_limit_bytes=VMEM_LIMIT),
        name="in_proj",
    )(x2, ln_g, w_in, qg, kg, bd, perm)


def _slope2(h):
    return LOG2E * 2.0 ** (-8.0 * (h + 1) / N_HEADS)


def _softmax_parts(q, kw, va, vb, bias_of_head):
    tq = q.shape[0]
    lane = lax.broadcasted_iota(jnp.int32, (tq, LANES), 1)
    first_head = lane < HEAD_DIM
    heads = range(N_HEADS)
    pair = lambda t, h: t[:, (h // 2) * LANES:(h // 2 + 1) * LANES]
    own = [first_head if h % 2 == 0 else jnp.logical_not(first_head) for h in heads]
    v_own = [pair(va if h % 2 == 0 else vb, h) for h in heads]
    qm = [jnp.where(own[h], pair(q, h), 0.0).astype(BF16) for h in heads]
    s = [_mm_nt(qm[h], pair(kw, h)) + bias_of_head(h) for h in heads]
    m = [jnp.max(s[h], axis=-1, keepdims=True) for h in heads]
    p = [jnp.exp2(s[h] - m[h]).astype(BF16) for h in heads]
    pv = [_mm(p[h], v_own[h]) for h in heads]
    mx = jnp.zeros((tq, LANES), F32)
    for h in heads:
        mx = jnp.where(lane == h, m[h], mx)
    acc = [jnp.where(first_head, pv[h], pv[h + 1]) for h in heads[::2]]
    den = [pltpu.roll(jnp.where(first_head, pv[h + 1], pv[h]), HEAD_DIM, 1) for h in heads[::2]]
    return jnp.concatenate(acc, axis=1), jnp.concatenate(den, axis=1), mx


def _attn_far_kernel(q_ref, k_ref, va_ref, vb_ref, acc_ref, den_ref, mx_ref):
    sub_len = q_ref.shape[0]
    edge = lambda t: jnp.concatenate([t[:RADIUS], t, t[sub_len - RADIUS:]], axis=0)
    kcat, vacat, vbcat = edge(k_ref[...]), edge(va_ref[...]), edge(vb_ref[...])
    rows = lax.broadcasted_iota(jnp.int32, (ATTN_TQ, ATTN_TK), 0)
    cols = lax.broadcasted_iota(jnp.int32, (ATTN_TQ, ATTN_TK), 1)
    dist = jnp.abs(cols - rows - RADIUS)
    for a in range(0, sub_len, ATTN_TQ):
        kidx = a - RADIUS + cols
        valid = (dist <= RADIUS) & (kidx >= 0) & (kidx < sub_len)
        base = jnp.where(valid, dist.astype(F32) * (-float(DIL_FAR)), NEG_INF)
        win = slice(a, a + ATTN_TK)
        acc, den, mx = _softmax_parts(q_ref[a:a + ATTN_TQ, :].astype(F32), kcat[win], vacat[win],
                                      vbcat[win], lambda h: _slope2(h) * base)
        acc_ref[a:a + ATTN_TQ, :] = acc
        den_ref[a:a + ATTN_TQ, :] = den
        mx_ref[a:a + ATTN_TQ, :] = mx


def _attn_far(q16, k16, va16, vb16):
    b, n_res, sub_len, _ = q16.shape
    spec = pl.BlockSpec((None, None, sub_len, WIDTH), lambda bi, r: (bi, r, 0, 0))
    mx_spec = pl.BlockSpec((None, None, sub_len, LANES), lambda bi, r: (bi, r, 0, 0))
    wide = jax.ShapeDtypeStruct((b, n_res, sub_len, WIDTH), F32)
    return pl.pallas_call(
        _attn_far_kernel,
        grid=(b, n_res),
        in_specs=[spec, spec, spec, spec],
        out_specs=[spec, spec, mx_spec],
        out_shape=[wide, wide, jax.ShapeDtypeStruct((b, n_res, sub_len, LANES), F32)],
        compiler_params=pltpu.CompilerParams(dimension_semantics=("parallel", "parallel"),
                                             vmem_limit_bytes=VMEM_LIMIT),
        name="attn_far",
    )(q16, k16, va16, vb16)


def _attn_near_kernel(q_ref, k_ref, va_ref, vb_ref, acc_far_ref, den_far_ref, mx_far_ref, bias_ref,
                      o_ref):
    seq = k_ref.shape[0]
    t0 = pl.program_id(1) * ATTN_TQ
    w0 = pl.multiple_of(jnp.clip(t0 - NEAR_REACH, 0, seq - NEAR_WIN), ATTN_TQ)
    win = pl.ds(w0, NEAR_WIN)

    r1 = lax.broadcasted_iota(jnp.int32, (ATTN_TQ, ATTN_TQ), 0)
    c1 = lax.broadcasted_iota(jnp.int32, (ATTN_TQ, ATTN_TQ), 1)
    per = ATTN_TQ // DIL_FAR
    shift = per.bit_length() - 1
    pos_of = lambda row: DIL_FAR * (row & (per - 1)) + (row >> shift)
    to_state = jnp.where(c1 == pos_of(r1), 1.0, 0.0).astype(BF16)
    to_natural = jnp.where(r1 == pos_of(c1), 1.0, 0.0).astype(BF16)
    q = _mm(to_state, q_ref[...])

    first_blk = (w0 - t0) // ATTN_TQ + NEAR_BIAS_SHIFT
    n_blk = NEAR_WIN // ATTN_TQ
    bias = lambda h: jnp.concatenate([bias_ref[h, first_blk + c] for c in range(n_blk)], axis=1)
    acc, den, mx = _softmax_parts(q, k_ref[win, :], va_ref[win, :], vb_ref[win, :], bias)

    mx_far = mx_far_ref[...].reshape(ATTN_TQ, LANES)
    m_max = jnp.maximum(mx, mx_far)
    er = lax.broadcasted_iota(jnp.int32, (LANES, WIDTH), 0)
    ec = lax.broadcasted_iota(jnp.int32, (LANES, WIDTH), 1)
    spread = jnp.where(er == ec >> (HEAD_DIM.bit_length() - 1), 1.0, 0.0).astype(BF16)

    def per_lane(w):
        hi = w.astype(BF16)
        lo = (w - hi.astype(F32)).astype(BF16)
        return _mm(hi, spread) + _mm(lo, spread)

    w_near = per_lane(jnp.exp2(mx - m_max))
    w_far = per_lane(jnp.exp2(mx_far - m_max))
    num = w_near * acc + w_far * acc_far_ref[...].reshape(ATTN_TQ, WIDTH)
    out = num / (w_near * den + w_far * den_far_ref[...].reshape(ATTN_TQ, WIDTH))
    o_ref[...] = _mm(to_natural, out.astype(BF16)).astype(o_ref.dtype)


def _near_bias_table():
    per = ATTN_TQ // DIL_FAR
    row = np.arange(ATTN_TQ)
    pos = DIL_FAR * (row % per) + row // per
    n_cols = NEAR_WIN + NEAR_BIAS_SHIFT * ATTN_TQ
    delta = (np.arange(n_cols)[None, :] - NEAR_BIAS_SHIFT * ATTN_TQ) - pos[:, None]
    dist = np.abs(delta)
    count = (dist <= RADIUS * ATTN_BRANCHES[0][1]).astype(np.int32)
    count += (delta % ATTN_BRANCHES[1][1] == 0) & (dist <= NEAR_REACH)
    slopes = np.array([_slope2(h) for h in range(N_HEADS)], np.float64)
    with np.errstate(divide="ignore"):
        tab = -slopes[:, None, None] * dist[None] + np.log2(count)[None]
    tab = np.where(count[None] > 0, tab, NEG_INF).astype(np.float32)
    return tab.reshape(N_HEADS, ATTN_TQ, n_cols // ATTN_TQ, ATTN_TQ).transpose(0, 2, 1, 3)


def _attn_near(q, k, va, vb, far, bias):
    b, s, _ = q.shape
    per = ATTN_TQ // DIL_FAR
    whole = pl.BlockSpec((None, s, WIDTH), lambda bi, i: (bi, 0, 0))
    blk = lambda width: pl.BlockSpec((None, DIL_FAR, per, width), lambda bi, i: (bi, 0, i, 0))
    return pl.pallas_call(
        _attn_near_kernel,
        grid=(b, s // ATTN_TQ),
        in_specs=[pl.BlockSpec((None, ATTN_TQ, WIDTH), lambda bi, i: (bi, i, 0)), whole, whole, whole,
                  blk(WIDTH), blk(WIDTH), blk(LANES),
                  pl.BlockSpec(bias.shape, lambda bi, i: (0, 0, 0, 0))],
        out_specs=pl.BlockSpec((None, ATTN_TQ, WIDTH), lambda bi, i: (bi, i, 0)),
        out_shape=jax.ShapeDtypeStruct((b, s, WIDTH), BF16),
        compiler_params=pltpu.CompilerParams(dimension_semantics=("parallel", "parallel"),
                                             vmem_limit_bytes=VMEM_LIMIT),
        name="attn_near",
    )(q, k, va, vb, *far, bias)


def _sigmoid(x):
    return 1.0 / (1.0 + jnp.exp(-x))


def _prep_kernel(u_ref, up_ref, un_ref, sp_ref, sn_ref, w0_ref, w2h_ref, w2l_ref, a0_ref, a2_ref,
                 g2_ref, kk_ref, ka_ref, rk_ref, bd_ref,
                 r_ref, v_ref, kap_ref, lw_ref, kt_ref, bb_ref, g_ref, bonus_ref):
    i = pl.program_id(1)
    nblk = pl.num_programs(1)
    u = u_ref[...]
    row = lax.broadcasted_iota(jnp.int32, u.shape, 0)
    prev_row = jnp.where(i > 0, up_ref[7:8, :], 0.0)
    next_row = jnp.where(i < nblk - 1, un_ref[0:1, :], 0.0)
    u_prev = jnp.where(row == 0, prev_row, pltpu.roll(u, 1, 0))
    u_next = jnp.where(row == PREP_ROWS - 1, next_row, pltpu.roll(u, PREP_ROWS - 1, 0))
    um = u + sp_ref[...] * (u_prev - u) + sn_ref[...] * (u_next - u)

    bd = bd_ref[...]
    r = um[:, 0:WIDTH]
    k = um[:, WIDTH:2 * WIDTH]
    v = um[:, 2 * WIDTH:3 * WIDTH]
    lora = um[:, 3 * WIDTH:3 * WIDTH + LORA_BLOCK]
    xg = um[:, 3 * WIDTH + LORA_BLOCK:]
    r_ref[...] = r.astype(BF16)
    v_ref[...] = v.astype(BF16)
    g_ref[...] = _mm(_sigmoid(xg).astype(BF16), g2_ref[...]).astype(BF16)
    kk = k * kk_ref[...]
    kap = kk * lax.rsqrt(_seg_mean(kk * kk, bd) * HEAD_DIM + 1e-12)
    kap_ref[...] = kap.astype(BF16)
    tw = jnp.tanh(lora)
    tw_hi = tw.astype(BF16)
    tw_lo = (tw - tw_hi.astype(F32)).astype(BF16)
    lora16 = lora.astype(BF16)
    kt_sum = jnp.zeros_like(k)
    for d in range(2):
        z = w0_ref[d:d + 1, :] + (_mm(tw_hi, w2h_ref[d])
                                  + (_mm(tw_lo, w2h_ref[d]) + _mm(tw_hi, w2l_ref[d])))
        softplus = jnp.maximum(-z, 0.0) + jnp.log(1.0 + jnp.exp(-jnp.abs(z)))
        lw_ref[d] = -jnp.exp(-softplus - 0.5)
        a = _sigmoid(a0_ref[d:d + 1, :] + _mm(lora16, a2_ref[d]))
        kt = k * (1.0 + (a - 1.0) * ka_ref[...])
        kt_ref[d] = kt.astype(BF16)
        bb_ref[d] = (a * kap).astype(BF16)
        kt_sum = kt_sum + kt
    bonus_ref[...] = (_seg_mean(r * kt_sum * rk_ref[...], bd) * HEAD_DIM * v).astype(BF16)


def _rwkv_prep(u, p, layer, bd):
    b, s, _ = u.shape
    nblk = s // PREP_ROWS
    halo = PREP_ROWS // 8
    cur = lambda bi, i: (bi, i, 0)
    par = lambda bi, i: (layer, 0, 0)
    par3 = par
    par4 = lambda bi, i: (layer, 0, 0, 0)
    out_cur = pl.BlockSpec((None, PREP_ROWS, WIDTH), cur)
    out_dir = pl.BlockSpec((2, None, PREP_ROWS, WIDTH), lambda bi, i: (0, bi, i, 0))
    one = jax.ShapeDtypeStruct((b, s, WIDTH), BF16)
    two = jax.ShapeDtypeStruct((2, b, s, WIDTH), BF16)
    log_decay = jax.ShapeDtypeStruct((2, b, s, WIDTH), F32)
    return pl.pallas_call(
        _prep_kernel,
        grid=(b, nblk),
        in_specs=[
            pl.BlockSpec((None, PREP_ROWS, RWKV_PROJ), cur),
            pl.BlockSpec((None, 8, RWKV_PROJ), lambda bi, i: (bi, jnp.maximum(i * halo - 1, 0), 0)),
            pl.BlockSpec((None, 8, RWKV_PROJ),
                         lambda bi, i: (bi, jnp.minimum((i + 1) * halo, s // 8 - 1), 0)),
            pl.BlockSpec((None, 1, RWKV_PROJ), par),
            pl.BlockSpec((None, 1, RWKV_PROJ), par),
            pl.BlockSpec((None, 2, WIDTH), par3),
            pl.BlockSpec((None, 2, LORA_BLOCK, WIDTH), par4),
            pl.BlockSpec((None, 2, LORA_BLOCK, WIDTH), par4),
            pl.BlockSpec((None, 2, WIDTH), par3),
            pl.BlockSpec((None, 2, LORA_BLOCK, WIDTH), par4),
            pl.BlockSpec((None, GATE_LORA, WIDTH), par3),
            pl.BlockSpec((None, 1, WIDTH), par),
            pl.BlockSpec((None, 1, WIDTH), par),
            pl.BlockSpec((None, 1, WIDTH), par),
            pl.BlockSpec((WIDTH, WIDTH), lambda bi, i: (0, 0)),
        ],
        out_specs=[out_cur, out_cur, out_cur, out_dir, out_dir, out_dir, out_cur, out_cur],
        out_shape=[one, one, one, log_decay, two, two, one, one],
        compiler_params=pltpu.CompilerParams(dimension_semantics=("parallel", "parallel"),
                                             vmem_limit_bytes=VMEM_LIMIT),
        name="rwkv_prep",
    )(u, u, u, p["tshift_prev"], p["tshift_next"], p["w0"], p["w2_hi"], p["w2_lo"], p["a0"], p["a2"],
      p["g2"], p["k_k"], p["k_a"], p["r_k"], bd)


def _wkv_kernel(*refs):
    in_refs = (refs[0:6], refs[6:12])
    y_refs = refs[12:14]
    s_ref = refs[14]

    @pl.when(pl.program_id(1) == 0)
    def _():
        s_ref[...] = jnp.zeros_like(s_ref)

    tt = lax.broadcasted_iota(jnp.int32, (CHUNK, LANES), 0)
    lane = lax.broadcasted_iota(jnp.int32, (CHUNK, LANES), 1)
    jj = lane & (HEAD_DIM - 1)
    first_head = lane < HEAD_DIM
    keep_a = jnp.where(first_head, 1.0, 0.0).astype(BF16)
    keep_b = jnp.where(first_head, 0.0, 1.0).astype(BF16)
    rows2 = lax.broadcasted_iota(jnp.int32, (LANES, LANES), 0) < HEAD_DIM
    cols2 = lax.broadcasted_iota(jnp.int32, (LANES, LANES), 1) < HEAD_DIM
    diag_blocks = rows2 == cols2
    eye = (tt == jj).astype(F32)
    same16 = (tt >> 4) == (jj >> 4)
    same32 = (tt >> 5) == (jj >> 5)
    b16 = lambda t: t.astype(BF16)
    bdiag = lambda t: jnp.concatenate([b16(t) * keep_a, b16(t) * keep_b], axis=0)
    mm = lambda x, y: _mm(b16(x), b16(y))
    mm_nt = lambda x, y: _mm_nt(b16(x), b16(y))
    mm_tn = lambda x, y: _mm_tn(b16(x), b16(y))
    pmm = lambda x, y: mm(x, bdiag(y))
    cut = lambda t: [t[:, p * LANES:(p + 1) * LANES] for p in range(N_HEADS // 2)]
    t_small = lax.broadcasted_iota(jnp.int32, (CHUNK, CHUNK), 0)
    j_small = lax.broadcasted_iota(jnp.int32, (CHUNK, CHUNK), 1)

    qt_h, rt_h, v_h, bh_h, kb_h, bbar_h, kh_h, pc_h = ([] for _ in range(8))
    strict, incl, in16, in32, in64 = ([] for _ in range(5))
    n_pair = N_HEADS // 2
    for d, dir_refs in enumerate(in_refs):
        before = (jj < tt) if d == 0 else (jj > tt)
        upto = jnp.logical_or(before, tt == jj)
        tri = jnp.where((j_small <= t_small) if d == 0 else (j_small >= t_small), 1.0, 0.0)
        tri = tri.astype(BF16)
        for bi in range(WKV_BATCH):
            r, v, kap, lw, kt, bb = (ref[bi] for ref in dir_refs)
            r, kap, kt, bb = (t.astype(F32) for t in (r, kap, kt, bb))
            lw_hi = b16(lw)
            rest = lw - lw_hi.astype(F32)
            lw_mid = b16(rest)
            lw_lo = b16(rest - lw_mid.astype(F32))
            cs = _mm(tri, lw_hi) + (_mm(tri, lw_mid) + _mm(tri, lw_lo))
            tot = jnp.sum(lw, axis=0, keepdims=True)
            e_neg = jnp.exp(-cs)
            e_tail = jnp.exp(tot - cs)
            qt_h += cut(kap * jnp.exp(cs - lw))
            rt_h += cut(r * jnp.exp(cs))
            v_h += cut(v)
            kb_h += cut(b16(kt * e_neg))
            bbar_h += cut(b16(bb * e_neg))
            kh_h += cut(b16(kt * e_tail))
            bh_h += cut(b16(bb * e_tail))
            pc_h += cut(jnp.exp(tot))
            strict += [before] * n_pair
            incl += [upto] * n_pair
            in16 += [before & same16] * n_pair
            in32 += [before & same32 & jnp.logical_not(same16)] * n_pair
            in64 += [before & jnp.logical_not(same32)] * n_pair

    each = lambda f, *lists: [f(*args) for args in zip(*lists)]
    qr = each(lambda q, r: b16(jnp.concatenate([q, r], axis=0)), qt_h, rt_h)
    a_kb = each(lambda q, k, bb: mm_nt(q, jnp.concatenate([bdiag(k), bdiag(bb)], axis=0)),
                qr, kb_h, bbar_h)
    a_k = [t[:, :LANES] for t in a_kb]
    a_qb = [t[:CHUNK, LANES:] for t in a_kb]
    a_rb = each(lambda m, t: b16(jnp.where(m, t[CHUNK:, LANES:], 0.0)), incl, a_kb)

    n1 = each(lambda m, t: jnp.where(m, t, 0.0), in16, a_qb)
    n2 = each(pmm, n1, n1)
    t_inv = [eye - t for t in n1]
    pw = n2
    for _ in range(2):
        both = each(lambda p, t: pmm(jnp.concatenate([p, t], axis=0), p), pw, t_inv)
        t_inv = each(lambda t, bt: t + bt[CHUNK:], t_inv, both)
        pw = [bt[:CHUNK] for bt in both]
    t_inv = each(lambda t, p: t + pmm(t, p), t_inv, pw)
    for sel in (in32, in64):
        t16 = [b16(t) for t in t_inv]
        tmp = each(lambda t, m, a: pmm(t, jnp.where(m, a, 0.0)), t16, sel, a_qb)
        t_inv = each(lambda t, m, t6: t - pmm(m, t6), t_inv, tmp, t16)

    a_kc = each(lambda ms, mi, t: jnp.concatenate(
        [jnp.where(ms, t[:CHUNK], 0.0), jnp.where(mi, t[CHUNK:], 0.0)], axis=0), strict, incl, a_k)
    av = each(pmm, a_kc, v_h)
    x = each(lambda t, q, a: mm(t, jnp.concatenate([bdiag(q), bdiag(a[:CHUNK])], axis=1)),
             t_inv, qt_h, av)
    xq = [b16(t[:, :LANES]) for t in x]
    xu = [b16(t[:, LANES:]) for t in x]
    w = each(lambda a, q, u: mm(a, jnp.concatenate([bdiag(q), bdiag(u)], axis=1)), a_rb, xq, xu)
    rh = each(lambda r, ww: b16(r - ww[:, :LANES]), rt_h, w)
    y0 = each(lambda a, ww: a[CHUNK:] - ww[:, LANES:], av, w)
    m1 = each(lambda q, bh: b16(jnp.where(diag_blocks, mm_tn(q, bh), 0.0)), xq, bh_h)
    vk2 = each(lambda vv, u, kh, bh: mm_tn(jnp.concatenate([vv, -u], axis=0),
                                           jnp.concatenate([kh, bh], axis=0)), v_h, xu, kh_h, bh_h)
    vk = [jnp.where(first_head, t[:HEAD_DIM], t[HEAD_DIM:]) for t in vk2]
    n_chain = 2 * WKV_BATCH * n_pair
    st = [s_ref[i] for i in range(n_chain)]
    st16 = [b16(t) for t in st]
    ys = each(lambda r, s6, y: mm_nt(r, bdiag(s6)) + y, rh, st16, y0)
    sm = each(mm, st16, m1)
    for i in range(n_chain):
        s_ref[i] = st[i] * pc_h[i] - sm[i] + vk[i]
    for d, y_ref in enumerate(y_refs):
        for bi in range(WKV_BATCH):
            first = (d * WKV_BATCH + bi) * n_pair
            y_ref[bi] = jnp.concatenate(ys[first:first + n_pair], axis=1).astype(y_ref.dtype)


def _wkv(r, v, kap, lw, kt, bb):
    b, s, _ = r.shape
    nc = s // CHUNK
    specs = []
    for d in range(2):
        pos = (lambda c: c) if d == 0 else (lambda c: nc - 1 - c)
        shared = pl.BlockSpec((WKV_BATCH, CHUNK, WIDTH), lambda bi, c, pos=pos: (bi, pos(c), 0))
        direc = pl.BlockSpec((None, WKV_BATCH, CHUNK, WIDTH),
                             lambda bi, c, pos=pos, d=d: (d, bi, pos(c), 0))
        specs.append((shared, direc))
    in_specs = [sp for shared, direc in specs for sp in (shared,) * 3 + (direc,) * 3]
    y_shape = jax.ShapeDtypeStruct((b, s, WIDTH), BF16)
    n_chain = 2 * WKV_BATCH * (N_HEADS // 2)
    return pl.pallas_call(
        _wkv_kernel,
        grid=(b // WKV_BATCH, nc),
        in_specs=in_specs,
        out_specs=[specs[0][0], specs[1][0]],
        out_shape=[y_shape, y_shape],
        scratch_shapes=[pltpu.VMEM((n_chain, HEAD_DIM, LANES), F32)],
        compiler_params=pltpu.CompilerParams(
            dimension_semantics=("parallel", "arbitrary"), vmem_limit_bytes=VMEM_LIMIT),
        name="wkv7",
    )(*((r, v, kap, lw, kt, bb) * 2))


def _outproj_kernel(x_ref, attn_ref, yf_ref, yr_ref, bonus_ref, g_ref, gw_ref, gb_ref, bd_ref, w_ref,
                    o_ref):
    bd = bd_ref[...]
    y = yf_ref[...].astype(F32) + yr_ref[...].astype(F32)
    yc = y - _seg_mean(y, bd, split=False)
    yn = yc * lax.rsqrt(_seg_mean(yc * yc, bd, split=False) + GN_EPS) * gw_ref[...] + gb_ref[...]
    rw = ((yn + bonus_ref[...].astype(F32)) * g_ref[...].astype(F32)).astype(BF16)
    o_ref[...] = x_ref[...] + _mm(attn_ref[...], w_ref[0:WIDTH, :]) + _mm(rw, w_ref[WIDTH:, :])


def _out_proj(x2, attn2, yf2, yr2, bonus2, g2, gn_w, gn_b, bd, w_out, layer):
    t = x2.shape[0]
    row = lambda i: (i, 0)
    par = lambda i: (layer, 0, 0)
    return pl.pallas_call(
        _outproj_kernel,
        grid=(t // ROW_TILE,),
        in_specs=[
            pl.BlockSpec((ROW_TILE, D_MODEL), row),
            pl.BlockSpec((ROW_TILE, WIDTH), row),
            pl.BlockSpec((ROW_TILE, WIDTH), row),
            pl.BlockSpec((ROW_TILE, WIDTH), row),
            pl.BlockSpec((ROW_TILE, WIDTH), row),
            pl.BlockSpec((ROW_TILE, WIDTH), row),
            pl.BlockSpec((None, 1, WIDTH), par),
            pl.BlockSpec((None, 1, WIDTH), par),
            pl.BlockSpec((WIDTH, WIDTH), lambda i: (0, 0)),
            pl.BlockSpec((None, D_MODEL, D_MODEL), lambda i: (layer, 0, 0)),
        ],
        out_specs=pl.BlockSpec((ROW_TILE, D_MODEL), row),
        out_shape=jax.ShapeDtypeStruct((t, D_MODEL), F32),
        compiler_params=pltpu.CompilerParams(dimension_semantics=("parallel",),
                                             vmem_limit_bytes=VMEM_LIMIT),
        name="out_proj",
    )(x2, attn2, yf2, yr2, bonus2, g2, gn_w, gn_b, bd, w_out)


def _mlp_kernel(x_ref, g_ref, wu_ref, wd_ref, o_ref):
    x = x_ref[...]
    h = _rms_rows(x, g_ref[...]).astype(BF16)
    acc = x
    for c in range(0, D_FF, D_MODEL):
        a = jnp.maximum(_mm(h, wu_ref[:, c:c + D_MODEL]), 0.0)
        acc = acc + _mm((a * a).astype(BF16), wd_ref[c:c + D_MODEL, :])
    o_ref[...] = acc


def _mlp(x2, ln_g, w_up, w_down, layer):
    t = x2.shape[0]
    row = lambda i: (i, 0)
    return pl.pallas_call(
        _mlp_kernel,
        grid=(t // ROW_TILE,),
        in_specs=[
            pl.BlockSpec((ROW_TILE, D_MODEL), row),
            pl.BlockSpec((None, 1, D_MODEL), lambda i: (layer, 0, 0)),
            pl.BlockSpec((None, D_MODEL, D_FF), lambda i: (layer, 0, 0)),
            pl.BlockSpec((None, D_FF, D_MODEL), lambda i: (layer, 0, 0)),
        ],
        out_specs=pl.BlockSpec((ROW_TILE, D_MODEL), row),
        out_shape=jax.ShapeDtypeStruct((t, D_MODEL), F32),
        compiler_params=pltpu.CompilerParams(dimension_semantics=("parallel",),
                                             vmem_limit_bytes=VMEM_LIMIT),
        name="mlp",
    )(x2, ln_g, w_up, w_down)


def _pad_lora(w, lo):
    pad = [(0, 0), (0, 0), (lo, LORA_BLOCK - lo - w.shape[2]), (0, 0)]
    return jnp.pad(w, pad)


def kernel(x, ln1_g, w_in, q_norm_g, k_norm_g, tshift_prev, tshift_next, rwkv_w0, rwkv_w2, rwkv_a0,
           rwkv_a2, rwkv_g2, rwkv_k_k, rwkv_k_a, rwkv_r_k, rwkv_gn_w, rwkv_gn_b, w_out, ln2_g, w_up,
           w_down):
    b, s, dm = x.shape
    depth = w_in.shape[0]
    t = b * s
    head_of = np.arange(WIDTH) // HEAD_DIM
    bd = jnp.asarray(head_of[:, None] == head_of[None, :], BF16)
    out_row = np.arange(PERM_ROWS)
    src_row = DIL_FAR * (out_row % (PERM_ROWS // DIL_FAR)) + out_row // (PERM_ROWS // DIL_FAR)
    perm = jnp.asarray(src_row[:, None] == np.arange(PERM_ROWS)[None, :], BF16)
    near_bias = jnp.asarray(_near_bias_table())

    w_in_b = w_in.astype(BF16)
    w_out_b = w_out.astype(BF16)
    w_up_b = w_up.astype(BF16)
    w_down_b = w_down.astype(BF16)
    vec = lambda p: p.reshape(depth, 1, -1)
    qg = vec(jnp.tile(q_norm_g, (1, N_HEADS)))
    kg = vec(jnp.tile(k_norm_g, (1, N_HEADS)))
    ln1_g, ln2_g, rwkv_gn_w, rwkv_gn_b = vec(ln1_g), vec(ln2_g), vec(rwkv_gn_w), vec(rwkv_gn_b)
    decay_cols = rwkv_w2.shape[2]
    w2_pad = _pad_lora(rwkv_w2, 0)
    w2_hi = w2_pad.astype(BF16)
    prep_params = {
        "tshift_prev": vec(tshift_prev), "tshift_next": vec(tshift_next),
        "w0": rwkv_w0, "w2_hi": w2_hi, "w2_lo": (w2_pad - w2_hi.astype(F32)).astype(BF16),
        "a0": rwkv_a0, "a2": _pad_lora(rwkv_a2, decay_cols).astype(BF16),
        "g2": rwkv_g2.astype(BF16), "k_k": vec(rwkv_k_k), "k_a": vec(rwkv_k_a),
        "r_k": vec(rwkv_r_k),
    }

    x2 = x.reshape(t, dm)
    for layer in range(depth):
        q, k, va, vb, q16, k16, va16, vb16, u = _in_proj(x2, ln1_g, w_in_b, qg, kg, bd, perm, layer, b)
        far = _attn_far(q16, k16, va16, vb16)
        attn = _attn_near(*(a.reshape(b, s, WIDTH) for a in (q, k, va, vb)), far, near_bias)
        r, rv, kap, lw, kt, bb, g, bonus = _rwkv_prep(u.reshape(b, s, RWKV_PROJ), prep_params,
                                                      layer, bd)
        yf, yr = _wkv(r, rv, kap, lw, kt, bb)
        flat = lambda a: a.reshape(t, WIDTH)
        x2 = _out_proj(x2, flat(attn), flat(yf), flat(yr), flat(bonus), flat(g), rwkv_gn_w,
                       rwkv_gn_b, bd, w_out_b, layer)
        x2 = _mlp(x2, ln2_g, w_up_b, w_down_b, layer)
    return x2.reshape(b, s, dm)
```

```python
import functools
import math

import numpy as np
import jax
import jax.numpy as jnp
from jax import lax
from jax.experimental import pallas as pl
from jax.experimental.pallas import tpu as pltpu

D_MODEL = 1024
HEAD_DIM = 64
N_HEADS = 8
WIDTH = N_HEADS * HEAD_DIM
ATTN_BRANCHES = ((128, 1), (512, 4), (2048, 16))
RADIUS = 64
LORA_BLOCK = 128
GATE_LORA = 128
RWKV_PROJ = 3 * WIDTH + LORA_BLOCK + GATE_LORA
D_FF = 4 * D_MODEL
NORM_EPS = 1e-6
GN_EPS = HEAD_DIM * 1e-5
NEG_INF = -1e30

LANES = 128
ROW_TILE = 512
PERM_ROWS = 256
ATTN_TQ = 128
ATTN_TK = ATTN_TQ + 2 * RADIUS
DIL_FAR = ATTN_BRANCHES[2][1]
NEAR_REACH = RADIUS * ATTN_BRANCHES[1][1]
NEAR_WIN = ATTN_TQ + 2 * NEAR_REACH
NEAR_BIAS_SHIFT = (NEAR_WIN - ATTN_TQ) // ATTN_TQ
LOG2E = 1.4426950408889634
MIX_ROWS = 256
CHUNK = 64
WKV_BATCH = 2
VMEM_LIMIT = 56 * 1024 * 1024

F32 = jnp.float32
BF16 = jnp.bfloat16


def _mm(a, b, prec=None):
    return lax.dot_general(a, b, (((1,), (0,)), ((), ())), precision=prec,
                           preferred_element_type=F32)


def _mm_nt(a, b, prec=None):
    return lax.dot_general(a, b, (((1,), (1,)), ((), ())), precision=prec,
                           preferred_element_type=F32)


def _mm_tn(a, b, prec=None):
    return lax.dot_general(a, b, (((0,), (0,)), ((), ())), precision=prec,
                           preferred_element_type=F32)


def _seg_mean(x, bd, split=True):
    hi = x.astype(BF16)
    total = _mm(hi, bd)
    if split:
        total = total + _mm((x - hi.astype(F32)).astype(BF16), bd)
    return total * (1.0 / HEAD_DIM)


def _rms_rows(x, g):
    return x * lax.rsqrt(jnp.mean(x * x, axis=-1, keepdims=True) + NORM_EPS) * g


def _inproj_kernel(x_ref, xp_ref, xn_ref, g_ref, w_ref, qg_ref, kg_ref, bd_ref, perm_ref, *refs,
                   tiles_per_seq):
    rwkv_params, outs = refs[:11], refs[11:]
    q_ref, k_ref, va_ref, vb_ref, q16_ref, k16_ref, va16_ref, vb16_ref = outs[:8]
    x_all = jnp.concatenate([x_ref[...], xp_ref[...], xn_ref[...]], axis=0)
    h_all = _rms_rows(x_all, g_ref[...]).astype(BF16)
    h = h_all[:MIX_ROWS]
    bd = bd_ref[...]
    u_all = _mm(h_all, w_ref[:, 3 * WIDTH:])
    q = _mm(h, w_ref[:, 0:WIDTH])
    k = _mm(h, w_ref[:, WIDTH:2 * WIDTH])
    v = _mm(h, w_ref[:, 2 * WIDTH:3 * WIDTH])
    tile = pl.program_id(0) % tiles_per_seq
    halo = xp_ref.shape[0]
    prev_row = jnp.where(tile > 0, u_all[MIX_ROWS + halo - 1:MIX_ROWS + halo], 0.0)
    next_row = jnp.where(tile < tiles_per_seq - 1, u_all[MIX_ROWS + halo:MIX_ROWS + halo + 1], 0.0)
    _rwkv_columns(u_all[:MIX_ROWS], prev_row, next_row, bd, *rwkv_params, *outs[8:])

    q = q * (lax.rsqrt(_seg_mean(q * q, bd, split=False) + NORM_EPS) * qg_ref[...]
             * (HEAD_DIM ** -0.5 * LOG2E))
    k = k * lax.rsqrt(_seg_mean(k * k, bd, split=False) + NORM_EPS) * kg_ref[...]
    even_head = (lax.broadcasted_iota(jnp.int32, v.shape, 1) & HEAD_DIM) == 0
    va = jnp.where(even_head, v, 1.0)
    vb = jnp.where(even_head, 1.0, v)
    perm = perm_ref[...]
    for t, nat_ref, res_ref in ((q, q_ref, q16_ref), (k, k_ref, k16_ref), (va, va_ref, va16_ref),
                                (vb, vb_ref, vb16_ref)):
        t = t.astype(BF16)
        nat_ref[...] = t
        per = PERM_ROWS // DIL_FAR
        for part in range(MIX_ROWS // PERM_ROWS):
            grouped = _mm(perm, t[part * PERM_ROWS:(part + 1) * PERM_ROWS]).astype(BF16)
            res_ref[:, part * per:(part + 1) * per, :] = grouped.reshape(DIL_FAR, per, WIDTH)


def _in_proj(x2, ln_g, w_in, qg, kg, bd, perm, p, layer, batch):
    t = x2.shape[0]
    n_in = w_in.shape[-1]
    seq = t // batch
    tiles = seq // MIX_ROWS
    halo = 8
    per_tile = MIX_ROWS // halo
    row = lambda i: (i, 0)
    par = lambda i: (layer, 0, 0)
    par4 = lambda i: (layer, 0, 0, 0)
    res_spec = pl.BlockSpec((None, DIL_FAR, MIX_ROWS // DIL_FAR, WIDTH),
                            lambda i: (i // tiles, 0, i % tiles, 0))
    nat_spec = pl.BlockSpec((MIX_ROWS, WIDTH), row)
    dir_spec = pl.BlockSpec((2, MIX_ROWS, WIDTH), lambda i: (0, i, 0))
    nat = jax.ShapeDtypeStruct((t, WIDTH), BF16)
    res = jax.ShapeDtypeStruct((batch, DIL_FAR, seq // DIL_FAR, WIDTH), BF16)
    both = jax.ShapeDtypeStruct((2, t, WIDTH), BF16)
    return pl.pallas_call(
        functools.partial(_inproj_kernel, tiles_per_seq=tiles),
        grid=(t // MIX_ROWS,),
        in_specs=[
            pl.BlockSpec((MIX_ROWS, D_MODEL), row),
            pl.BlockSpec((halo, D_MODEL), lambda i: (jnp.maximum(i * per_tile - 1, 0), 0)),
            pl.BlockSpec((halo, D_MODEL), lambda i: (jnp.minimum((i + 1) * per_tile, t // halo - 1), 0)),
            pl.BlockSpec((None, 1, D_MODEL), par),
            pl.BlockSpec((None, D_MODEL, n_in), par),
            pl.BlockSpec((None, 1, WIDTH), par),
            pl.BlockSpec((None, 1, WIDTH), par),
            pl.BlockSpec((WIDTH, WIDTH), lambda i: (0, 0)),
            pl.BlockSpec((PERM_ROWS, PERM_ROWS), lambda i: (0, 0)),
            pl.BlockSpec((None, 1, RWKV_PROJ), par),
            pl.BlockSpec((None, 1, RWKV_PROJ), par),
            pl.BlockSpec((None, 2, WIDTH), par),
            pl.BlockSpec((None, 2, LORA_BLOCK, WIDTH), par4),
            pl.BlockSpec((None, 2, LORA_BLOCK, WIDTH), par4),
            pl.BlockSpec((None, 2, WIDTH), par),
            pl.BlockSpec((None, 2, LORA_BLOCK, WIDTH), par4),
            pl.BlockSpec((None, GATE_LORA, WIDTH), par),
            pl.BlockSpec((None, 1, WIDTH), par),
            pl.BlockSpec((None, 1, WIDTH), par),
            pl.BlockSpec((None, 1, WIDTH), par),
        ],
        out_specs=[nat_spec] * 4 + [res_spec] * 4 + [nat_spec] * 3 + [dir_spec] * 3 + [nat_spec] * 2,
        out_shape=([nat] * 4 + [res] * 4 + [nat] * 3
                   + [jax.ShapeDtypeStruct((2, t, WIDTH), F32), both, both] + [nat] * 2),
        compiler_params=pltpu.CompilerParams(dimension_semantics=("parallel",),
                                             vmem_limit_bytes=VMEM_LIMIT),
        name="in_proj",
    )(x2, x2, x2, ln_g, w_in, qg, kg, bd, perm, p["tshift_prev"], p["tshift_next"], p["w0"],
      p["w2_hi"], p["w2_lo"], p["a0"], p["a2"], p["g2"], p["k_k"], p["k_a"], p["r_k"])


def _slope2(h):
    return LOG2E * 2.0 ** (-8.0 * (h + 1) / N_HEADS)


def _softmax_parts(q, kw, va, vb, bias_of_head):
    tq = q.shape[0]
    lane = lax.broadcasted_iota(jnp.int32, (tq, LANES), 1)
    first_head = lane < HEAD_DIM
    heads = range(N_HEADS)
    pair = lambda t, h: t[:, (h // 2) * LANES:(h // 2 + 1) * LANES]
    own = [first_head if h % 2 == 0 else jnp.logical_not(first_head) for h in heads]
    v_own = [pair(va if h % 2 == 0 else vb, h) for h in heads]
    qm = [jnp.where(own[h], pair(q, h), 0.0).astype(BF16) for h in heads]
    s = [_mm_nt(qm[h], pair(kw, h)) + bias_of_head(h) for h in heads]
    m = [jnp.max(s[h], axis=-1, keepdims=True) for h in heads]
    p = [jnp.exp2(s[h] - m[h]).astype(BF16) for h in heads]
    pv = [_mm(p[h], v_own[h]) for h in heads]
    mx = jnp.zeros((tq, LANES), F32)
    for h in heads:
        mx = jnp.where(lane == h, m[h], mx)
    acc = [jnp.where(first_head, pv[h], pv[h + 1]) for h in heads[::2]]
    den = [pltpu.roll(jnp.where(first_head, pv[h + 1], pv[h]), HEAD_DIM, 1) for h in heads[::2]]
    return jnp.concatenate(acc, axis=1), jnp.concatenate(den, axis=1), mx


def _attn_far_kernel(q_ref, k_ref, va_ref, vb_ref, acc_ref, den_ref, mx_ref):
    sub_len = q_ref.shape[0]
    edge = lambda t: jnp.concatenate([t[:RADIUS], t, t[sub_len - RADIUS:]], axis=0)
    kcat, vacat, vbcat = edge(k_ref[...]), edge(va_ref[...]), edge(vb_ref[...])
    rows = lax.broadcasted_iota(jnp.int32, (ATTN_TQ, ATTN_TK), 0)
    cols = lax.broadcasted_iota(jnp.int32, (ATTN_TQ, ATTN_TK), 1)
    dist = jnp.abs(cols - rows - RADIUS)
    for a in range(0, sub_len, ATTN_TQ):
        kidx = a - RADIUS + cols
        valid = (dist <= RADIUS) & (kidx >= 0) & (kidx < sub_len)
        base = jnp.where(valid, dist.astype(F32) * (-float(DIL_FAR)), NEG_INF)
        win = slice(a, a + ATTN_TK)
        acc, den, mx = _softmax_parts(q_ref[a:a + ATTN_TQ, :].astype(F32), kcat[win], vacat[win],
                                      vbcat[win], lambda h: _slope2(h) * base)
        acc_ref[a:a + ATTN_TQ, :] = acc
        den_ref[a:a + ATTN_TQ, :] = den
        mx_ref[a:a + ATTN_TQ, :] = mx


def _attn_far(q16, k16, va16, vb16):
    b, n_res, sub_len, _ = q16.shape
    spec = pl.BlockSpec((None, None, sub_len, WIDTH), lambda bi, r: (bi, r, 0, 0))
    mx_spec = pl.BlockSpec((None, None, sub_len, LANES), lambda bi, r: (bi, r, 0, 0))
    wide = jax.ShapeDtypeStruct((b, n_res, sub_len, WIDTH), F32)
    return pl.pallas_call(
        _attn_far_kernel,
        grid=(b, n_res),
        in_specs=[spec, spec, spec, spec],
        out_specs=[spec, spec, mx_spec],
        out_shape=[wide, wide, jax.ShapeDtypeStruct((b, n_res, sub_len, LANES), F32)],
        compiler_params=pltpu.CompilerParams(dimension_semantics=("parallel", "parallel"),
                                             vmem_limit_bytes=VMEM_LIMIT),
        name="attn_far",
    )(q16, k16, va16, vb16)


def _attn_near_kernel(q_ref, k_ref, va_ref, vb_ref, acc_far_ref, den_far_ref, mx_far_ref, bias_ref,
                      o_ref):
    seq = k_ref.shape[0]
    t0 = pl.program_id(1) * ATTN_TQ
    w0 = pl.multiple_of(jnp.clip(t0 - NEAR_REACH, 0, seq - NEAR_WIN), ATTN_TQ)
    win = pl.ds(w0, NEAR_WIN)

    r1 = lax.broadcasted_iota(jnp.int32, (ATTN_TQ, ATTN_TQ), 0)
    c1 = lax.broadcasted_iota(jnp.int32, (ATTN_TQ, ATTN_TQ), 1)
    per = ATTN_TQ // DIL_FAR
    shift = per.bit_length() - 1
    pos_of = lambda row: DIL_FAR * (row & (per - 1)) + (row >> shift)
    to_state = jnp.where(c1 == pos_of(r1), 1.0, 0.0).astype(BF16)
    to_natural = jnp.where(r1 == pos_of(c1), 1.0, 0.0).astype(BF16)
    q = _mm(to_state, q_ref[...])

    first_blk = (w0 - t0) // ATTN_TQ + NEAR_BIAS_SHIFT
    n_blk = NEAR_WIN // ATTN_TQ
    bias = lambda h: jnp.concatenate([bias_ref[h, first_blk + c] for c in range(n_blk)], axis=1)
    acc, den, mx = _softmax_parts(q, k_ref[win, :], va_ref[win, :], vb_ref[win, :], bias)

    mx_far = mx_far_ref[...].reshape(ATTN_TQ, LANES)
    m_max = jnp.maximum(mx, mx_far)
    er = lax.broadcasted_iota(jnp.int32, (LANES, WIDTH), 0)
    ec = lax.broadcasted_iota(jnp.int32, (LANES, WIDTH), 1)
    spread = jnp.where(er == ec >> (HEAD_DIM.bit_length() - 1), 1.0, 0.0).astype(BF16)

    def per_lane(w):
        hi = w.astype(BF16)
        lo = (w - hi.astype(F32)).astype(BF16)
        return _mm(hi, spread) + _mm(lo, spread)

    w_near = per_lane(jnp.exp2(mx - m_max))
    w_far = per_lane(jnp.exp2(mx_far - m_max))
    num = w_near * acc + w_far * acc_far_ref[...].reshape(ATTN_TQ, WIDTH)
    out = num / (w_near * den + w_far * den_far_ref[...].reshape(ATTN_TQ, WIDTH))
    o_ref[...] = _mm(to_natural, out.astype(BF16)).astype(o_ref.dtype)


def _near_bias_table():
    per = ATTN_TQ // DIL_FAR
    row = np.arange(ATTN_TQ)
    pos = DIL_FAR * (row % per) + row // per
    n_cols = NEAR_WIN + NEAR_BIAS_SHIFT * ATTN_TQ
    delta = (np.arange(n_cols)[None, :] - NEAR_BIAS_SHIFT * ATTN_TQ) - pos[:, None]
    dist = np.abs(delta)
    count = (dist <= RADIUS * ATTN_BRANCHES[0][1]).astype(np.int32)
    count += (delta % ATTN_BRANCHES[1][1] == 0) & (dist <= NEAR_REACH)
    slopes = np.array([_slope2(h) for h in range(N_HEADS)], np.float64)
    with np.errstate(divide="ignore"):
        tab = -slopes[:, None, None] * dist[None] + np.log2(count)[None]
    tab = np.where(count[None] > 0, tab, NEG_INF).astype(np.float32)
    return tab.reshape(N_HEADS, ATTN_TQ, n_cols // ATTN_TQ, ATTN_TQ).transpose(0, 2, 1, 3)


def _attn_near(q, k, va, vb, far, bias):
    b, s, _ = q.shape
    per = ATTN_TQ // DIL_FAR
    whole = pl.BlockSpec((None, s, WIDTH), lambda bi, i: (bi, 0, 0))
    blk = lambda width: pl.BlockSpec((None, DIL_FAR, per, width), lambda bi, i: (bi, 0, i, 0))
    return pl.pallas_call(
        _attn_near_kernel,
        grid=(b, s // ATTN_TQ),
        in_specs=[pl.BlockSpec((None, ATTN_TQ, WIDTH), lambda bi, i: (bi, i, 0)), whole, whole, whole,
                  blk(WIDTH), blk(WIDTH), blk(LANES),
                  pl.BlockSpec(bias.shape, lambda bi, i: (0, 0, 0, 0))],
        out_specs=pl.BlockSpec((None, ATTN_TQ, WIDTH), lambda bi, i: (bi, i, 0)),
        out_shape=jax.ShapeDtypeStruct((b, s, WIDTH), BF16),
        compiler_params=pltpu.CompilerParams(dimension_semantics=("parallel", "parallel"),
                                             vmem_limit_bytes=VMEM_LIMIT),
        name="attn_near",
    )(q, k, va, vb, *far, bias)


def _sigmoid(x):
    return 1.0 / (1.0 + jnp.exp(-x))


def _rwkv_columns(u, prev_row, next_row, bd, sp_ref, sn_ref, w0_ref, w2h_ref, w2l_ref, a0_ref, a2_ref,
                  g2_ref, kk_ref, ka_ref, rk_ref,
                  r_ref, v_ref, kap_ref, lw_ref, kt_ref, bb_ref, g_ref, bonus_ref):
    rows = u.shape[0]
    row = lax.broadcasted_iota(jnp.int32, u.shape, 0)
    u_prev = jnp.where(row == 0, prev_row, pltpu.roll(u, 1, 0))
    u_next = jnp.where(row == rows - 1, next_row, pltpu.roll(u, rows - 1, 0))
    um = u + sp_ref[...] * (u_prev - u) + sn_ref[...] * (u_next - u)

    r = um[:, 0:WIDTH]
    k = um[:, WIDTH:2 * WIDTH]
    v = um[:, 2 * WIDTH:3 * WIDTH]
    lora = um[:, 3 * WIDTH:3 * WIDTH + LORA_BLOCK]
    xg = um[:, 3 * WIDTH + LORA_BLOCK:]
    r_ref[...] = r.astype(BF16)
    v_ref[...] = v.astype(BF16)
    g_ref[...] = _mm(_sigmoid(xg).astype(BF16), g2_ref[...]).astype(BF16)
    kk = k * kk_ref[...]
    kap = kk * lax.rsqrt(_seg_mean(kk * kk, bd) * HEAD_DIM + 1e-12)
    kap_ref[...] = kap.astype(BF16)
    tw = jnp.tanh(lora)
    tw_hi = tw.astype(BF16)
    tw_lo = (tw - tw_hi.astype(F32)).astype(BF16)
    lora16 = lora.astype(BF16)
    kt_sum = jnp.zeros_like(k)
    for d in range(2):
        z = w0_ref[d:d + 1, :] + (_mm(tw_hi, w2h_ref[d])
                                  + (_mm(tw_lo, w2h_ref[d]) + _mm(tw_hi, w2l_ref[d])))
        lw_ref[d] = _sigmoid(z) * (-math.exp(-0.5))
        a = _sigmoid(a0_ref[d:d + 1, :] + _mm(lora16, a2_ref[d]))
        kt = k * (1.0 + (a - 1.0) * ka_ref[...])
        kt_ref[d] = kt.astype(BF16)
        bb_ref[d] = (a * kap).astype(BF16)
        kt_sum = kt_sum + kt
    bonus_ref[...] = (_seg_mean(r * kt_sum * rk_ref[...], bd) * HEAD_DIM * v).astype(BF16)


def _wkv_kernel(*refs):
    in_refs = (refs[0:6], refs[6:12])
    y_refs = refs[12:14]
    s_ref = refs[14]

    @pl.when(pl.program_id(1) == 0)
    def _():
        s_ref[...] = jnp.zeros_like(s_ref)

    tt = lax.broadcasted_iota(jnp.int32, (CHUNK, LANES), 0)
    lane = lax.broadcasted_iota(jnp.int32, (CHUNK, LANES), 1)
    jj = lane & (HEAD_DIM - 1)
    first_head = lane < HEAD_DIM
    keep_a = jnp.where(first_head, 1.0, 0.0).astype(BF16)
    keep_b = jnp.where(first_head, 0.0, 1.0).astype(BF16)
    rows2 = lax.broadcasted_iota(jnp.int32, (LANES, LANES), 0) < HEAD_DIM
    cols2 = lax.broadcasted_iota(jnp.int32, (LANES, LANES), 1) < HEAD_DIM
    diag_blocks = rows2 == cols2
    eye = (tt == jj).astype(F32)
    same16 = (tt >> 4) == (jj >> 4)
    same32 = (tt >> 5) == (jj >> 5)
    b16 = lambda t: t.astype(BF16)
    bdiag = lambda t: jnp.concatenate([b16(t) * keep_a, b16(t) * keep_b], axis=0)
    mm = lambda x, y: _mm(b16(x), b16(y))
    mm_nt = lambda x, y: _mm_nt(b16(x), b16(y))
    mm_tn = lambda x, y: _mm_tn(b16(x), b16(y))
    pmm = lambda x, y: mm(x, bdiag(y))
    cut = lambda t: [t[:, p * LANES:(p + 1) * LANES] for p in range(N_HEADS // 2)]
    t_small = lax.broadcasted_iota(jnp.int32, (CHUNK, CHUNK), 0)
    j_small = lax.broadcasted_iota(jnp.int32, (CHUNK, CHUNK), 1)

    qt_h, rt_h, v_h, bh_h, kb_h, bbar_h, kh_h, pc_h = ([] for _ in range(8))
    strict, incl, in16, in32, in64 = ([] for _ in range(5))
    n_pair = N_HEADS // 2
    for d, dir_refs in enumerate(in_refs):
        before = (jj < tt) if d == 0 else (jj > tt)
        upto = jnp.logical_or(before, tt == jj)
        tri = jnp.where((j_small <= t_small) if d == 0 else (j_small >= t_small), 1.0, 0.0)
        tri = tri.astype(BF16)
        for bi in range(WKV_BATCH):
            r, v, kap, lw, kt, bb = (ref[bi] for ref in dir_refs)
            r, kap, kt, bb = (t.astype(F32) for t in (r, kap, kt, bb))
            lw_hi = b16(lw)
            rest = lw - lw_hi.astype(F32)
            lw_mid = b16(rest)
            lw_lo = b16(rest - lw_mid.astype(F32))
            cs = _mm(tri, lw_hi) + (_mm(tri, lw_mid) + _mm(tri, lw_lo))
            tot = jnp.sum(lw, axis=0, keepdims=True)
            e_neg = jnp.exp(-cs)
            e_tail = jnp.exp(tot - cs)
            qt_h += cut(kap * jnp.exp(cs - lw))
            rt_h += cut(r * jnp.exp(cs))
            v_h += cut(v)
            kb_h += cut(b16(kt * e_neg))
            bbar_h += cut(b16(bb * e_neg))
            kh_h += cut(b16(kt * e_tail))
            bh_h += cut(b16(bb * e_tail))
            pc_h += cut(jnp.exp(tot))
            strict += [before] * n_pair
            incl += [upto] * n_pair
            in16 += [before & same16] * n_pair
            in32 += [before & same32 & jnp.logical_not(same16)] * n_pair
            in64 += [before & jnp.logical_not(same32)] * n_pair

    each = lambda f, *lists: [f(*args) for args in zip(*lists)]
    qr = each(lambda q, r: b16(jnp.concatenate([q, r], axis=0)), qt_h, rt_h)
    a_kb = each(lambda q, k, bb: mm_nt(q, jnp.concatenate([bdiag(k), bdiag(bb)], axis=0)),
                qr, kb_h, bbar_h)
    a_k = [t[:, :LANES] for t in a_kb]
    a_qb = [t[:CHUNK, LANES:] for t in a_kb]
    a_rb = each(lambda m, t: b16(jnp.where(m, t[CHUNK:, LANES:], 0.0)), incl, a_kb)

    n1 = each(lambda m, t: jnp.where(m, t, 0.0), in16, a_qb)
    n2 = each(pmm, n1, n1)
    t_inv = [eye - t for t in n1]
    pw = n2
    for _ in range(2):
        both = each(lambda p, t: pmm(jnp.concatenate([p, t], axis=0), p), pw, t_inv)
        t_inv = each(lambda t, bt: t + bt[CHUNK:], t_inv, both)
        pw = [bt[:CHUNK] for bt in both]
    t_inv = each(lambda t, p: t + pmm(t, p), t_inv, pw)
    for sel in (in32, in64):
        t16 = [b16(t) for t in t_inv]
        tmp = each(lambda t, m, a: pmm(t, jnp.where(m, a, 0.0)), t16, sel, a_qb)
        t_inv = each(lambda t, m, t6: t - pmm(m, t6), t_inv, tmp, t16)

    a_kc = each(lambda ms, mi, t: jnp.concatenate(
        [jnp.where(ms, t[:CHUNK], 0.0), jnp.where(mi, t[CHUNK:], 0.0)], axis=0), strict, incl, a_k)
    av = each(pmm, a_kc, v_h)
    x = each(lambda t, q, a: mm(t, jnp.concatenate([bdiag(q), bdiag(a[:CHUNK])], axis=1)),
             t_inv, qt_h, av)
    xq = [b16(t[:, :LANES]) for t in x]
    xu = [b16(t[:, LANES:]) for t in x]
    w = each(lambda a, q, u: mm(a, jnp.concatenate([bdiag(q), bdiag(u)], axis=1)), a_rb, xq, xu)
    rh = each(lambda r, ww: b16(r - ww[:, :LANES]), rt_h, w)
    y0 = each(lambda a, ww: a[CHUNK:] - ww[:, LANES:], av, w)
    m1 = each(lambda q, bh: b16(jnp.where(diag_blocks, mm_tn(q, bh), 0.0)), xq, bh_h)
    vk2 = each(lambda vv, u, kh, bh: mm_tn(jnp.concatenate([vv, -u], axis=0),
                                           jnp.concatenate([kh, bh], axis=0)), v_h, xu, kh_h, bh_h)
    vk = [jnp.where(first_head, t[:HEAD_DIM], t[HEAD_DIM:]) for t in vk2]
    n_chain = 2 * WKV_BATCH * n_pair
    st = [s_ref[i] for i in range(n_chain)]
    st16 = [b16(t) for t in st]
    ys = each(lambda r, s6, y: mm_nt(r, bdiag(s6)) + y, rh, st16, y0)
    sm = each(mm, st16, m1)
    for i in range(n_chain):
        s_ref[i] = st[i] * pc_h[i] - sm[i] + vk[i]
    for d, y_ref in enumerate(y_refs):
        for bi in range(WKV_BATCH):
            first = (d * WKV_BATCH + bi) * n_pair
            y_ref[bi] = jnp.concatenate(ys[first:first + n_pair], axis=1).astype(y_ref.dtype)


def _wkv(r, v, kap, lw, kt, bb):
    b, s, _ = r.shape
    nc = s // CHUNK
    specs = []
    for d in range(2):
        pos = (lambda c: c) if d == 0 else (lambda c: nc - 1 - c)
        shared = pl.BlockSpec((WKV_BATCH, CHUNK, WIDTH), lambda bi, c, pos=pos: (bi, pos(c), 0))
        direc = pl.BlockSpec((None, WKV_BATCH, CHUNK, WIDTH),
                             lambda bi, c, pos=pos, d=d: (d, bi, pos(c), 0))
        specs.append((shared, direc))
    in_specs = [sp for shared, direc in specs for sp in (shared,) * 3 + (direc,) * 3]
    y_shape = jax.ShapeDtypeStruct((b, s, WIDTH), BF16)
    n_chain = 2 * WKV_BATCH * (N_HEADS // 2)
    return pl.pallas_call(
        _wkv_kernel,
        grid=(b // WKV_BATCH, nc),
        in_specs=in_specs,
        out_specs=[specs[0][0], specs[1][0]],
        out_shape=[y_shape, y_shape],
        scratch_shapes=[pltpu.VMEM((n_chain, HEAD_DIM, LANES), F32)],
        compiler_params=pltpu.CompilerParams(
            dimension_semantics=("parallel", "arbitrary"), vmem_limit_bytes=VMEM_LIMIT),
        name="wkv7",
    )(*((r, v, kap, lw, kt, bb) * 2))


def _channel_kernel(x_ref, attn_ref, yf_ref, yr_ref, bonus_ref, g_ref, gw_ref, gb_ref, bd_ref, w_ref,
                    ln_ref, wu_ref, wd_ref, o_ref):
    bd = bd_ref[...]
    y = yf_ref[...].astype(F32) + yr_ref[...].astype(F32)
    yc = y - _seg_mean(y, bd, split=False)
    yn = yc * lax.rsqrt(_seg_mean(yc * yc, bd, split=False) + GN_EPS) * gw_ref[...] + gb_ref[...]
    rw = ((yn + bonus_ref[...].astype(F32)) * g_ref[...].astype(F32)).astype(BF16)
    o_ref[...] = x_ref[...] + _mm(attn_ref[...], w_ref[0:WIDTH, :]) + _mm(rw, w_ref[WIDTH:, :])
    x = o_ref[...]
    h = _rms_rows(x, ln_ref[...]).astype(BF16)
    acc = x
    for c in range(0, D_FF, D_MODEL):
        a = jnp.maximum(_mm(h, wu_ref[:, c:c + D_MODEL]), 0.0)
        acc = acc + _mm((a * a).astype(BF16), wd_ref[c:c + D_MODEL, :])
    o_ref[...] = acc


def _channel_mix(x2, attn2, yf2, yr2, bonus2, g2, gn_w, gn_b, bd, w_out, ln_g, w_up, w_down, layer):
    t = x2.shape[0]
    row = lambda i: (i, 0)
    par = lambda i: (layer, 0, 0)
    resident = lambda shape: pl.BlockSpec(shape, par)
    return pl.pallas_call(
        _channel_kernel,
        grid=(t // ROW_TILE,),
        in_specs=[
            pl.BlockSpec((ROW_TILE, D_MODEL), row),
            pl.BlockSpec((ROW_TILE, WIDTH), row),
            pl.BlockSpec((ROW_TILE, WIDTH), row),
            pl.BlockSpec((ROW_TILE, WIDTH), row),
            pl.BlockSpec((ROW_TILE, WIDTH), row),
            pl.BlockSpec((ROW_TILE, WIDTH), row),
            pl.BlockSpec((None, 1, WIDTH), par),
            pl.BlockSpec((None, 1, WIDTH), par),
            pl.BlockSpec((WIDTH, WIDTH), lambda i: (0, 0)),
            resident((None, D_MODEL, D_MODEL)),
            pl.BlockSpec((None, 1, D_MODEL), par),
            resident((None, D_MODEL, D_FF)),
            resident((None, D_FF, D_MODEL)),
        ],
        out_specs=pl.BlockSpec((ROW_TILE, D_MODEL), row),
        out_shape=jax.ShapeDtypeStruct((t, D_MODEL), F32),
        compiler_params=pltpu.CompilerParams(dimension_semantics=("parallel",),
                                             vmem_limit_bytes=VMEM_LIMIT),
        name="channel_mix",
    )(x2, attn2, yf2, yr2, bonus2, g2, gn_w, gn_b, bd, w_out, ln_g, w_up, w_down)


def _pad_lora(w, lo):
    pad = [(0, 0), (0, 0), (lo, LORA_BLOCK - lo - w.shape[2]), (0, 0)]
    return jnp.pad(w, pad)


def kernel(x, ln1_g, w_in, q_norm_g, k_norm_g, tshift_prev, tshift_next, rwkv_w0, rwkv_w2, rwkv_a0,
           rwkv_a2, rwkv_g2, rwkv_k_k, rwkv_k_a, rwkv_r_k, rwkv_gn_w, rwkv_gn_b, w_out, ln2_g, w_up,
           w_down):
    b, s, dm = x.shape
    depth = w_in.shape[0]
    t = b * s
    head_of = np.arange(WIDTH) // HEAD_DIM
    bd = jnp.asarray(head_of[:, None] == head_of[None, :], BF16)
    out_row = np.arange(PERM_ROWS)
    src_row = DIL_FAR * (out_row % (PERM_ROWS // DIL_FAR)) + out_row // (PERM_ROWS // DIL_FAR)
    perm = jnp.asarray(src_row[:, None] == np.arange(PERM_ROWS)[None, :], BF16)
    near_bias = jnp.asarray(_near_bias_table())

    w_in_b = w_in.astype(BF16)
    w_out_b = w_out.astype(BF16)
    w_up_b = w_up.astype(BF16)
    w_down_b = w_down.astype(BF16)
    vec = lambda p: p.reshape(depth, 1, -1)
    qg = vec(jnp.tile(q_norm_g, (1, N_HEADS)))
    kg = vec(jnp.tile(k_norm_g, (1, N_HEADS)))
    ln1_g, ln2_g, rwkv_gn_w, rwkv_gn_b = vec(ln1_g), vec(ln2_g), vec(rwkv_gn_w), vec(rwkv_gn_b)
    decay_cols = rwkv_w2.shape[2]
    w2_pad = _pad_lora(rwkv_w2, 0)
    w2_hi = w2_pad.astype(BF16)
    prep_params = {
        "tshift_prev": vec(tshift_prev), "tshift_next": vec(tshift_next),
        "w0": rwkv_w0, "w2_hi": w2_hi, "w2_lo": (w2_pad - w2_hi.astype(F32)).astype(BF16),
        "a0": rwkv_a0, "a2": _pad_lora(rwkv_a2, decay_cols).astype(BF16),
        "g2": rwkv_g2.astype(BF16), "k_k": vec(rwkv_k_k), "k_a": vec(rwkv_k_a),
        "r_k": vec(rwkv_r_k),
    }

    x2 = x.reshape(t, dm)
    for layer in range(depth):
        (q, k, va, vb, q16, k16, va16, vb16, r, rv, kap, lw, kt, bb, g, bonus) = _in_proj(
            x2, ln1_g, w_in_b, qg, kg, bd, perm, prep_params, layer, b)
        seq3 = lambda a: a.reshape(b, s, WIDTH)
        both = lambda a: a.reshape(2, b, s, WIDTH)
        flat = lambda a: a.reshape(t, WIDTH)
        far = _attn_far(q16, k16, va16, vb16)
        attn = _attn_near(seq3(q), seq3(k), seq3(va), seq3(vb), far, near_bias)
        yf, yr = _wkv(seq3(r), seq3(rv), seq3(kap), both(lw), both(kt), both(bb))
        x2 = _channel_mix(x2, flat(attn), flat(yf), flat(yr), bonus, g, rwkv_gn_w, rwkv_gn_b, bd,
                          w_out_b, ln2_g, w_up_b, w_down_b, layer)
    return x2.reshape(b, s, dm)
```

```python
import functools
import math

import numpy as np
import jax
import jax.numpy as jnp
from jax import lax
from jax.experimental import pallas as pl
from jax.experimental.pallas import tpu as pltpu

D_MODEL = 1024
HEAD_DIM = 64
N_HEADS = 8
WIDTH = N_HEADS * HEAD_DIM
ATTN_BRANCHES = ((128, 1), (512, 4), (2048, 16))
RADIUS = 64
LORA_BLOCK = 128
GATE_LORA = 128
RWKV_PROJ = 3 * WIDTH + LORA_BLOCK + GATE_LORA
D_FF = 4 * D_MODEL
NORM_EPS = 1e-6
GN_EPS = HEAD_DIM * 1e-5
NEG_INF = -1e30

LANES = 128
ROW_TILE = 512
PERM_ROWS = 256
ATTN_TQ = 128
ATTN_TK = ATTN_TQ + 2 * RADIUS
DIL_FAR = ATTN_BRANCHES[2][1]
NEAR_REACH = RADIUS * ATTN_BRANCHES[1][1]
NEAR_WIN = ATTN_TQ + 2 * NEAR_REACH
NEAR_BIAS_SHIFT = (NEAR_WIN - ATTN_TQ) // ATTN_TQ
LOG2E = 1.4426950408889634
MIX_ROWS = 256
CHUNK = 64
WKV_BATCH = 4
VMEM_LIMIT = 56 * 1024 * 1024

F32 = jnp.float32
BF16 = jnp.bfloat16


def _mm(a, b, prec=None):
    return lax.dot_general(a, b, (((1,), (0,)), ((), ())), precision=prec,
                           preferred_element_type=F32)


def _mm_nt(a, b, prec=None):
    return lax.dot_general(a, b, (((1,), (1,)), ((), ())), precision=prec,
                           preferred_element_type=F32)


def _mm_tn(a, b, prec=None):
    return lax.dot_general(a, b, (((0,), (0,)), ((), ())), precision=prec,
                           preferred_element_type=F32)


def _seg_mean(x, bd):
    return _mm(x.astype(BF16), bd) * (1.0 / HEAD_DIM)


def _rms_rows(x, g):
    return x * lax.rsqrt(jnp.mean(x * x, axis=-1, keepdims=True) + NORM_EPS) * g


def _inproj_kernel(x_ref, xp_ref, xn_ref, g_ref, w_ref, qg_ref, kg_ref, bd_ref, perm_ref, *refs,
                   tiles_per_seq):
    rwkv_params, outs = refs[:11], refs[11:]
    q_ref, k_ref, va_ref, vb_ref, q16_ref, k16_ref, va16_ref, vb16_ref = outs[:8]
    x_all = jnp.concatenate([x_ref[...], xp_ref[...], xn_ref[...]], axis=0)
    h_all = _rms_rows(x_all, g_ref[...]).astype(BF16)
    h = h_all[:MIX_ROWS]
    bd = bd_ref[...]
    u_all = _mm(h_all, w_ref[:, 3 * WIDTH:])
    q = _mm(h, w_ref[:, 0:WIDTH])
    tile = pl.program_id(0) % tiles_per_seq
    halo = xp_ref.shape[0]
    prev_row = jnp.where(tile > 0, u_all[MIX_ROWS + halo - 1:MIX_ROWS + halo], 0.0)
    next_row = jnp.where(tile < tiles_per_seq - 1, u_all[MIX_ROWS + halo:MIX_ROWS + halo + 1], 0.0)
    rwkv = _rwkv_columns(u_all[:MIX_ROWS], prev_row, next_row, bd, *rwkv_params, *outs[8:])
    next(rwkv)
    k = _mm(h, w_ref[:, WIDTH:2 * WIDTH])
    v = _mm(h, w_ref[:, 2 * WIDTH:3 * WIDTH])
    next(rwkv)
    q = q * (lax.rsqrt(_seg_mean(q * q, bd) + NORM_EPS) * qg_ref[...]
             * (HEAD_DIM ** -0.5 * LOG2E))
    k = k * lax.rsqrt(_seg_mean(k * k, bd) + NORM_EPS) * kg_ref[...]
    for _ in rwkv:
        pass
    even_head = (lax.broadcasted_iota(jnp.int32, v.shape, 1) & HEAD_DIM) == 0
    va = jnp.where(even_head, v, 1.0)
    vb = jnp.where(even_head, 1.0, v)
    perm = perm_ref[...]
    for t, nat_ref, res_ref in ((q, q_ref, q16_ref), (k, k_ref, k16_ref), (va, va_ref, va16_ref),
                                (vb, vb_ref, vb16_ref)):
        t = t.astype(BF16)
        nat_ref[...] = t
        per = PERM_ROWS // DIL_FAR
        for part in range(MIX_ROWS // PERM_ROWS):
            grouped = _mm(perm, t[part * PERM_ROWS:(part + 1) * PERM_ROWS]).astype(BF16)
            res_ref[:, part * per:(part + 1) * per, :] = grouped.reshape(DIL_FAR, per, WIDTH)


def _in_proj(x2, ln_g, w_in, qg, kg, bd, perm, p, layer, batch):
    t = x2.shape[0]
    n_in = w_in.shape[-1]
    seq = t // batch
    tiles = seq // MIX_ROWS
    halo = 8
    per_tile = MIX_ROWS // halo
    row = lambda i: (i, 0)
    par = lambda i: (layer, 0, 0)
    par4 = lambda i: (layer, 0, 0, 0)
    res_spec = pl.BlockSpec((None, DIL_FAR, MIX_ROWS // DIL_FAR, WIDTH),
                            lambda i: (i // tiles, 0, i % tiles, 0))
    nat_spec = pl.BlockSpec((MIX_ROWS, WIDTH), row)
    dir_spec = pl.BlockSpec((2, MIX_ROWS, WIDTH), lambda i: (0, i, 0))
    nat = jax.ShapeDtypeStruct((t, WIDTH), BF16)
    res = jax.ShapeDtypeStruct((batch, DIL_FAR, seq // DIL_FAR, WIDTH), BF16)
    both = jax.ShapeDtypeStruct((2, t, WIDTH), BF16)
    return pl.pallas_call(
        functools.partial(_inproj_kernel, tiles_per_seq=tiles),
        grid=(t // MIX_ROWS,),
        in_specs=[
            pl.BlockSpec((MIX_ROWS, D_MODEL), row),
            pl.BlockSpec((halo, D_MODEL), lambda i: (jnp.maximum(i * per_tile - 1, 0), 0)),
            pl.BlockSpec((halo, D_MODEL), lambda i: (jnp.minimum((i + 1) * per_tile, t // halo - 1), 0)),
            pl.BlockSpec((None, 1, D_MODEL), par),
            pl.BlockSpec((None, D_MODEL, n_in), par),
            pl.BlockSpec((None, 1, WIDTH), par),
            pl.BlockSpec((None, 1, WIDTH), par),
            pl.BlockSpec((WIDTH, WIDTH), lambda i: (0, 0)),
            pl.BlockSpec((PERM_ROWS, PERM_ROWS), lambda i: (0, 0)),
            pl.BlockSpec((None, 1, RWKV_PROJ), par),
            pl.BlockSpec((None, 1, RWKV_PROJ), par),
            pl.BlockSpec((None, 2, WIDTH), par),
            pl.BlockSpec((None, 2, LORA_BLOCK, WIDTH), par4),
            pl.BlockSpec((None, 2, LORA_BLOCK, WIDTH), par4),
            pl.BlockSpec((None, 2, WIDTH), par),
            pl.BlockSpec((None, 2, LORA_BLOCK, WIDTH), par4),
            pl.BlockSpec((None, GATE_LORA, WIDTH), par),
            pl.BlockSpec((None, 1, WIDTH), par),
            pl.BlockSpec((None, 1, WIDTH), par),
            pl.BlockSpec((None, 1, WIDTH), par),
        ],
        out_specs=[nat_spec] * 4 + [res_spec] * 4 + [nat_spec] * 3 + [dir_spec] * 3 + [nat_spec] * 2,
        out_shape=([nat] * 4 + [res] * 4 + [nat] * 3
                   + [jax.ShapeDtypeStruct((2, t, WIDTH), F32), both, both] + [nat] * 2),
        compiler_params=pltpu.CompilerParams(dimension_semantics=("parallel",),
                                             vmem_limit_bytes=VMEM_LIMIT),
        name="in_proj",
    )(x2, x2, x2, ln_g, w_in, qg, kg, bd, perm, p["tshift_prev"], p["tshift_next"], p["w0"],
      p["w2_hi"], p["w2_lo"], p["a0"], p["a2"], p["g2"], p["k_k"], p["k_a"], p["r_k"])


def _slope2(h):
    return LOG2E * 2.0 ** (-8.0 * (h + 1) / N_HEADS)


def _softmax_parts(q, kw, va, vb, bias_of_head):
    tq = q.shape[0]
    lane = lax.broadcasted_iota(jnp.int32, (tq, LANES), 1)
    first_head = lane < HEAD_DIM
    heads = range(N_HEADS)
    pair = lambda t, h: t[:, (h // 2) * LANES:(h // 2 + 1) * LANES]
    own = [first_head if h % 2 == 0 else jnp.logical_not(first_head) for h in heads]
    v_own = [pair(va if h % 2 == 0 else vb, h) for h in heads]
    qm = [jnp.where(own[h], pair(q, h), 0.0).astype(BF16) for h in heads]
    s = [_mm_nt(qm[h], pair(kw, h)) + bias_of_head(h) for h in heads]
    m = [jnp.max(s[h], axis=-1, keepdims=True) for h in heads]
    p = [jnp.exp2(s[h] - m[h]).astype(BF16) for h in heads]
    pv = [_mm(p[h], v_own[h]) for h in heads]
    mx = jnp.zeros((tq, LANES), F32)
    for h in heads:
        mx = jnp.where(lane == h, m[h], mx)
    acc = [jnp.where(first_head, pv[h], pv[h + 1]) for h in heads[::2]]
    den = [pltpu.roll(jnp.where(first_head, pv[h + 1], pv[h]), HEAD_DIM, 1) for h in heads[::2]]
    return jnp.concatenate(acc, axis=1), jnp.concatenate(den, axis=1), mx


def _attn_far_kernel(q_ref, k_ref, va_ref, vb_ref, acc_ref, den_ref, mx_ref):
    sub_len = q_ref.shape[0]
    edge = lambda t: jnp.concatenate([t[:RADIUS], t, t[sub_len - RADIUS:]], axis=0)
    kcat, vacat, vbcat = edge(k_ref[...]), edge(va_ref[...]), edge(vb_ref[...])
    rows = lax.broadcasted_iota(jnp.int32, (ATTN_TQ, ATTN_TK), 0)
    cols = lax.broadcasted_iota(jnp.int32, (ATTN_TQ, ATTN_TK), 1)
    dist = jnp.abs(cols - rows - RADIUS)
    for a in range(0, sub_len, ATTN_TQ):
        kidx = a - RADIUS + cols
        valid = (dist <= RADIUS) & (kidx >= 0) & (kidx < sub_len)
        base = jnp.where(valid, dist.astype(F32) * (-float(DIL_FAR)), NEG_INF)
        win = slice(a, a + ATTN_TK)
        acc, den, mx = _softmax_parts(q_ref[a:a + ATTN_TQ, :].astype(F32), kcat[win], vacat[win],
                                      vbcat[win], lambda h: _slope2(h) * base)
        acc_ref[a:a + ATTN_TQ, :] = acc
        den_ref[a:a + ATTN_TQ, :] = den
        mx_ref[a:a + ATTN_TQ, :] = mx


def _attn_far(q16, k16, va16, vb16):
    b, n_res, sub_len, _ = q16.shape
    spec = pl.BlockSpec((None, None, sub_len, WIDTH), lambda bi, r: (bi, r, 0, 0))
    mx_spec = pl.BlockSpec((None, None, sub_len, LANES), lambda bi, r: (bi, r, 0, 0))
    wide = jax.ShapeDtypeStruct((b, n_res, sub_len, WIDTH), F32)
    return pl.pallas_call(
        _attn_far_kernel,
        grid=(b, n_res),
        in_specs=[spec, spec, spec, spec],
        out_specs=[spec, spec, mx_spec],
        out_shape=[wide, wide, jax.ShapeDtypeStruct((b, n_res, sub_len, LANES), F32)],
        compiler_params=pltpu.CompilerParams(dimension_semantics=("parallel", "parallel"),
                                             vmem_limit_bytes=VMEM_LIMIT),
        name="attn_far",
    )(q16, k16, va16, vb16)


def _attn_near_kernel(q_ref, k_ref, va_ref, vb_ref, acc_far_ref, den_far_ref, mx_far_ref, bias_ref,
                      o_ref):
    seq = k_ref.shape[0]
    t0 = pl.program_id(1) * ATTN_TQ
    w0 = pl.multiple_of(jnp.clip(t0 - NEAR_REACH, 0, seq - NEAR_WIN), ATTN_TQ)
    win = pl.ds(w0, NEAR_WIN)

    r1 = lax.broadcasted_iota(jnp.int32, (ATTN_TQ, ATTN_TQ), 0)
    c1 = lax.broadcasted_iota(jnp.int32, (ATTN_TQ, ATTN_TQ), 1)
    per = ATTN_TQ // DIL_FAR
    shift = per.bit_length() - 1
    pos_of = lambda row: DIL_FAR * (row & (per - 1)) + (row >> shift)
    to_state = jnp.where(c1 == pos_of(r1), 1.0, 0.0).astype(BF16)
    to_natural = jnp.where(r1 == pos_of(c1), 1.0, 0.0).astype(BF16)
    q = _mm(to_state, q_ref[...])

    first_blk = (w0 - t0) // ATTN_TQ + NEAR_BIAS_SHIFT
    n_blk = NEAR_WIN // ATTN_TQ
    bias = lambda h: jnp.concatenate([bias_ref[h, first_blk + c] for c in range(n_blk)], axis=1)
    acc, den, mx = _softmax_parts(q, k_ref[win, :], va_ref[win, :], vb_ref[win, :], bias)

    mx_far = mx_far_ref[...].reshape(ATTN_TQ, LANES)
    m_max = jnp.maximum(mx, mx_far)
    er = lax.broadcasted_iota(jnp.int32, (LANES, WIDTH), 0)
    ec = lax.broadcasted_iota(jnp.int32, (LANES, WIDTH), 1)
    spread = jnp.where(er == ec >> (HEAD_DIM.bit_length() - 1), 1.0, 0.0).astype(BF16)

    def per_lane(w):
        hi = w.astype(BF16)
        lo = (w - hi.astype(F32)).astype(BF16)
        return _mm(hi, spread) + _mm(lo, spread)

    w_near = per_lane(jnp.exp2(mx - m_max))
    w_far = per_lane(jnp.exp2(mx_far - m_max))
    num = w_near * acc + w_far * acc_far_ref[...].reshape(ATTN_TQ, WIDTH)
    out = num / (w_near * den + w_far * den_far_ref[...].reshape(ATTN_TQ, WIDTH))
    o_ref[...] = _mm(to_natural, out.astype(BF16)).astype(o_ref.dtype)


def _near_bias_table():
    per = ATTN_TQ // DIL_FAR
    row = np.arange(ATTN_TQ)
    pos = DIL_FAR * (row % per) + row // per
    n_cols = NEAR_WIN + NEAR_BIAS_SHIFT * ATTN_TQ
    delta = (np.arange(n_cols)[None, :] - NEAR_BIAS_SHIFT * ATTN_TQ) - pos[:, None]
    dist = np.abs(delta)
    count = (dist <= RADIUS * ATTN_BRANCHES[0][1]).astype(np.int32)
    count += (delta % ATTN_BRANCHES[1][1] == 0) & (dist <= NEAR_REACH)
    slopes = np.array([_slope2(h) for h in range(N_HEADS)], np.float64)
    with np.errstate(divide="ignore"):
        tab = -slopes[:, None, None] * dist[None] + np.log2(count)[None]
    tab = np.where(count[None] > 0, tab, NEG_INF).astype(np.float32)
    return tab.reshape(N_HEADS, ATTN_TQ, n_cols // ATTN_TQ, ATTN_TQ).transpose(0, 2, 1, 3)


def _attn_near(q, k, va, vb, far, bias):
    b, s, _ = q.shape
    per = ATTN_TQ // DIL_FAR
    whole = pl.BlockSpec((None, s, WIDTH), lambda bi, i: (bi, 0, 0))
    blk = lambda width: pl.BlockSpec((None, DIL_FAR, per, width), lambda bi, i: (bi, 0, i, 0))
    return pl.pallas_call(
        _attn_near_kernel,
        grid=(b, s // ATTN_TQ),
        in_specs=[pl.BlockSpec((None, ATTN_TQ, WIDTH), lambda bi, i: (bi, i, 0)), whole, whole, whole,
                  blk(WIDTH), blk(WIDTH), blk(LANES),
                  pl.BlockSpec(bias.shape, lambda bi, i: (0, 0, 0, 0))],
        out_specs=pl.BlockSpec((None, ATTN_TQ, WIDTH), lambda bi, i: (bi, i, 0)),
        out_shape=jax.ShapeDtypeStruct((b, s, WIDTH), BF16),
        compiler_params=pltpu.CompilerParams(dimension_semantics=("parallel", "parallel"),
                                             vmem_limit_bytes=VMEM_LIMIT),
        name="attn_near",
    )(q, k, va, vb, *far, bias)


def _sigmoid(x):
    return 1.0 / (1.0 + jnp.exp(-x))


def _rwkv_columns(u, prev_row, next_row, bd, sp_ref, sn_ref, w0_ref, w2h_ref, w2l_ref, a0_ref, a2_ref,
                  g2_ref, kk_ref, ka_ref, rk_ref,
                  r_ref, v_ref, kap_ref, lw_ref, kt_ref, bb_ref, g_ref, bonus_ref):
    rows = u.shape[0]
    row = lax.broadcasted_iota(jnp.int32, u.shape, 0)
    u_prev = jnp.where(row == 0, prev_row, pltpu.roll(u, 1, 0))
    u_next = jnp.where(row == rows - 1, next_row, pltpu.roll(u, rows - 1, 0))
    um = u + sp_ref[...] * (u_prev - u) + sn_ref[...] * (u_next - u)

    r = um[:, 0:WIDTH]
    k = um[:, WIDTH:2 * WIDTH]
    v = um[:, 2 * WIDTH:3 * WIDTH]
    lora = um[:, 3 * WIDTH:3 * WIDTH + LORA_BLOCK]
    xg = um[:, 3 * WIDTH + LORA_BLOCK:]
    r_ref[...] = r.astype(BF16)
    v_ref[...] = v.astype(BF16)
    g_ref[...] = _mm(_sigmoid(xg).astype(BF16), g2_ref[...]).astype(BF16)
    kk = k * kk_ref[...]
    kap = kk * lax.rsqrt(_seg_mean(kk * kk, bd) * HEAD_DIM + 1e-12)
    kap_ref[...] = kap.astype(BF16)
    tw = jnp.tanh(lora)
    tw_hi = tw.astype(BF16)
    tw_lo = (tw - tw_hi.astype(F32)).astype(BF16)
    lora16 = lora.astype(BF16)
    kt_sum = jnp.zeros_like(k)
    for d in range(2):
        yield
        z = w0_ref[d:d + 1, :] + (_mm(tw_hi, w2h_ref[d])
                                  + (_mm(tw_lo, w2h_ref[d]) + _mm(tw_hi, w2l_ref[d])))
        lw_ref[d] = _sigmoid(z) * (-math.exp(-0.5))
        a = _sigmoid(a0_ref[d:d + 1, :] + _mm(lora16, a2_ref[d]))
        kt = k * (1.0 + (a - 1.0) * ka_ref[...])
        kt_ref[d] = kt.astype(BF16)
        bb_ref[d] = (a * kap).astype(BF16)
        kt_sum = kt_sum + kt
    bonus_ref[...] = (_seg_mean(r * kt_sum * rk_ref[...], bd) * HEAD_DIM * v).astype(BF16)


def _wkv_kernel(*refs):
    in_refs = (refs[0:6], refs[6:12])
    y_refs = refs[12:14]
    s_ref = refs[14]

    @pl.when(pl.program_id(1) == 0)
    def _():
        s_ref[...] = jnp.zeros_like(s_ref)

    tt = lax.broadcasted_iota(jnp.int32, (CHUNK, LANES), 0)
    lane = lax.broadcasted_iota(jnp.int32, (CHUNK, LANES), 1)
    jj = lane & (HEAD_DIM - 1)
    first_head = lane < HEAD_DIM
    keep_a = jnp.where(first_head, 1.0, 0.0).astype(BF16)
    keep_b = jnp.where(first_head, 0.0, 1.0).astype(BF16)
    rows2 = lax.broadcasted_iota(jnp.int32, (LANES, LANES), 0) < HEAD_DIM
    cols2 = lax.broadcasted_iota(jnp.int32, (LANES, LANES), 1) < HEAD_DIM
    diag_blocks = rows2 == cols2
    eye = (tt == jj).astype(F32)
    same16 = (tt >> 4) == (jj >> 4)
    same32 = (tt >> 5) == (jj >> 5)
    b16 = lambda t: t.astype(BF16)
    bdiag = lambda t: jnp.concatenate([b16(t) * keep_a, b16(t) * keep_b], axis=0)
    mm = lambda x, y: _mm(b16(x), b16(y))
    mm_nt = lambda x, y: _mm_nt(b16(x), b16(y))
    mm_tn = lambda x, y: _mm_tn(b16(x), b16(y))
    pmm = lambda x, y: mm(x, bdiag(y))
    cut = lambda t: [t[:, p * LANES:(p + 1) * LANES] for p in range(N_HEADS // 2)]
    t_small = lax.broadcasted_iota(jnp.int32, (CHUNK, CHUNK), 0)
    j_small = lax.broadcasted_iota(jnp.int32, (CHUNK, CHUNK), 1)

    qt_h, rt_h, v_h, bh_h, kb_h, bbar_h, kh_h, pc_h = ([] for _ in range(8))
    strict, incl, in16, in32, in64 = ([] for _ in range(5))
    n_pair = N_HEADS // 2
    for d, dir_refs in enumerate(in_refs):
        before = (jj < tt) if d == 0 else (jj > tt)
        upto = jnp.logical_or(before, tt == jj)
        tri = jnp.where((j_small <= t_small) if d == 0 else (j_small >= t_small), 1.0, 0.0)
        tri = tri.astype(BF16)
        for bi in range(WKV_BATCH):
            r, v, kap, lw, kt, bb = (ref[bi] for ref in dir_refs)
            r, kap, kt, bb = (t.astype(F32) for t in (r, kap, kt, bb))
            lw_hi = b16(lw)
            rest = lw - lw_hi.astype(F32)
            lw_mid = b16(rest)
            lw_lo = b16(rest - lw_mid.astype(F32))
            cs = _mm(tri, lw_hi) + (_mm(tri, lw_mid) + _mm(tri, lw_lo))
            tot = jnp.sum(lw, axis=0, keepdims=True)
            e_neg = jnp.exp(-cs)
            e_tail = jnp.exp(tot - cs)
            qt_h += cut(kap * jnp.exp(cs - lw))
            rt_h += cut(r * jnp.exp(cs))
            v_h += cut(v)
            kb_h += cut(b16(kt * e_neg))
            bbar_h += cut(b16(bb * e_neg))
            kh_h += cut(b16(kt * e_tail))
            bh_h += cut(b16(bb * e_tail))
            pc_h += cut(jnp.exp(tot))
            strict += [before] * n_pair
            incl += [upto] * n_pair
            in16 += [before & same16] * n_pair
            in32 += [before & same32 & jnp.logical_not(same16)] * n_pair
            in64 += [before & jnp.logical_not(same32)] * n_pair

    each = lambda f, *lists: [f(*args) for args in zip(*lists)]
    qr = each(lambda q, r: b16(jnp.concatenate([q, r], axis=0)), qt_h, rt_h)
    a_kb = each(lambda q, k, bb: mm_nt(q, jnp.concatenate([bdiag(k), bdiag(bb)], axis=0)),
                qr, kb_h, bbar_h)
    a_k = [t[:, :LANES] for t in a_kb]
    a_qb = [t[:CHUNK, LANES:] for t in a_kb]
    a_rb = each(lambda m, t: b16(jnp.where(m, t[CHUNK:, LANES:], 0.0)), incl, a_kb)

    n1 = each(lambda m, t: jnp.where(m, t, 0.0), in16, a_qb)
    n2 = each(pmm, n1, n1)
    t_inv = [eye - t for t in n1]
    pw = n2
    for _ in range(2):
        both = each(lambda p, t: pmm(jnp.concatenate([p, t], axis=0), p), pw, t_inv)
        t_inv = each(lambda t, bt: t + bt[CHUNK:], t_inv, both)
        pw = [bt[:CHUNK] for bt in both]
    t_inv = each(lambda t, p: t + pmm(t, p), t_inv, pw)
    for sel in (in32, in64):
        t16 = [b16(t) for t in t_inv]
        tmp = each(lambda t, m, a: pmm(t, jnp.where(m, a, 0.0)), t16, sel, a_qb)
        t_inv = each(lambda t, m, t6: t - pmm(m, t6), t_inv, tmp, t16)

    a_kc = each(lambda ms, mi, t: jnp.concatenate(
        [jnp.where(ms, t[:CHUNK], 0.0), jnp.where(mi, t[CHUNK:], 0.0)], axis=0), strict, incl, a_k)
    av = each(pmm, a_kc, v_h)
    x = each(lambda t, q, a: mm(t, jnp.concatenate([bdiag(q), bdiag(a[:CHUNK])], axis=1)),
             t_inv, qt_h, av)
    xq = [b16(t[:, :LANES]) for t in x]
    xu = [b16(t[:, LANES:]) for t in x]
    w = each(lambda a, q, u: mm(a, jnp.concatenate([bdiag(q), bdiag(u)], axis=1)), a_rb, xq, xu)
    rh = each(lambda r, ww: b16(r - ww[:, :LANES]), rt_h, w)
    y0 = each(lambda a, ww: a[CHUNK:] - ww[:, LANES:], av, w)
    m1 = each(lambda q, bh: b16(jnp.where(diag_blocks, mm_tn(q, bh), 0.0)), xq, bh_h)
    vk2 = each(lambda vv, u, kh, bh: mm_tn(jnp.concatenate([vv, -u], axis=0),
                                           jnp.concatenate([kh, bh], axis=0)), v_h, xu, kh_h, bh_h)
    vk = [jnp.where(first_head, t[:HEAD_DIM], t[HEAD_DIM:]) for t in vk2]
    n_chain = 2 * WKV_BATCH * n_pair
    st = [s_ref[i] for i in range(n_chain)]
    st16 = [b16(t) for t in st]
    ys = each(lambda r, s6, y: mm_nt(r, bdiag(s6)) + y, rh, st16, y0)
    sm = each(mm, st16, m1)
    for i in range(n_chain):
        s_ref[i] = st[i] * pc_h[i] - sm[i] + vk[i]
    for d, y_ref in enumerate(y_refs):
        for bi in range(WKV_BATCH):
            first = (d * WKV_BATCH + bi) * n_pair
            y_ref[bi] = jnp.concatenate(ys[first:first + n_pair], axis=1).astype(y_ref.dtype)


def _wkv(r, v, kap, lw, kt, bb):
    b, s, _ = r.shape
    nc = s // CHUNK
    specs = []
    for d in range(2):
        pos = (lambda c: c) if d == 0 else (lambda c: nc - 1 - c)
        shared = pl.BlockSpec((WKV_BATCH, CHUNK, WIDTH), lambda bi, c, pos=pos: (bi, pos(c), 0))
        direc = pl.BlockSpec((None, WKV_BATCH, CHUNK, WIDTH),
                             lambda bi, c, pos=pos, d=d: (d, bi, pos(c), 0))
        specs.append((shared, direc))
    in_specs = [sp for shared, direc in specs for sp in (shared,) * 3 + (direc,) * 3]
    y_shape = jax.ShapeDtypeStruct((b, s, WIDTH), BF16)
    n_chain = 2 * WKV_BATCH * (N_HEADS // 2)
    return pl.pallas_call(
        _wkv_kernel,
        grid=(b // WKV_BATCH, nc),
        in_specs=in_specs,
        out_specs=[specs[0][0], specs[1][0]],
        out_shape=[y_shape, y_shape],
        scratch_shapes=[pltpu.VMEM((n_chain, HEAD_DIM, LANES), F32)],
        compiler_params=pltpu.CompilerParams(
            dimension_semantics=("parallel", "arbitrary"), vmem_limit_bytes=VMEM_LIMIT),
        name="wkv7",
    )(*((r, v, kap, lw, kt, bb) * 2))


def _channel_kernel(x_ref, attn_ref, yf_ref, yr_ref, bonus_ref, g_ref, gw_ref, gb_ref, bd_ref, w_ref,
                    ln_ref, wu_ref, wd_ref, o_ref):
    bd = bd_ref[...]
    y = yf_ref[...].astype(F32) + yr_ref[...].astype(F32)
    yc = y - _seg_mean(y, bd)
    yn = yc * lax.rsqrt(_seg_mean(yc * yc, bd) + GN_EPS) * gw_ref[...] + gb_ref[...]
    rw = ((yn + bonus_ref[...].astype(F32)) * g_ref[...].astype(F32)).astype(BF16)
    o_ref[...] = x_ref[...] + _mm(attn_ref[...], w_ref[0:WIDTH, :]) + _mm(rw, w_ref[WIDTH:, :])
    x = o_ref[...]
    h = _rms_rows(x, ln_ref[...]).astype(BF16)
    acc = x
    for c in range(0, D_FF, D_MODEL):
        a = jnp.maximum(_mm(h, wu_ref[:, c:c + D_MODEL]), 0.0)
        acc = acc + _mm((a * a).astype(BF16), wd_ref[c:c + D_MODEL, :])
    o_ref[...] = acc


def _channel_mix(x2, attn2, yf2, yr2, bonus2, g2, gn_w, gn_b, bd, w_out, ln_g, w_up, w_down, layer):
    t = x2.shape[0]
    row = lambda i: (i, 0)
    par = lambda i: (layer, 0, 0)
    resident = lambda shape: pl.BlockSpec(shape, par)
    return pl.pallas_call(
        _channel_kernel,
        grid=(t // ROW_TILE,),
        in_specs=[
            pl.BlockSpec((ROW_TILE, D_MODEL), row),
            pl.BlockSpec((ROW_TILE, WIDTH), row),
            pl.BlockSpec((ROW_TILE, WIDTH), row),
            pl.BlockSpec((ROW_TILE, WIDTH), row),
            pl.BlockSpec((ROW_TILE, WIDTH), row),
            pl.BlockSpec((ROW_TILE, WIDTH), row),
            pl.BlockSpec((None, 1, WIDTH), par),
            pl.BlockSpec((None, 1, WIDTH), par),
            pl.BlockSpec((WIDTH, WIDTH), lambda i: (0, 0)),
            resident((None, D_MODEL, D_MODEL)),
            pl.BlockSpec((None, 1, D_MODEL), par),
            resident((None, D_MODEL, D_FF)),
            resident((None, D_FF, D_MODEL)),
        ],
        out_specs=pl.BlockSpec((ROW_TILE, D_MODEL), row),
        out_shape=jax.ShapeDtypeStruct((t, D_MODEL), F32),
        compiler_params=pltpu.CompilerParams(dimension_semantics=("parallel",),
                                             vmem_limit_bytes=VMEM_LIMIT),
        name="channel_mix",
    )(x2, attn2, yf2, yr2, bonus2, g2, gn_w, gn_b, bd, w_out, ln_g, w_up, w_down)


def _pad_lora(w, lo):
    pad = [(0, 0), (0, 0), (lo, LORA_BLOCK - lo - w.shape[2]), (0, 0)]
    return jnp.pad(w, pad)


def kernel(x, ln1_g, w_in, q_norm_g, k_norm_g, tshift_prev, tshift_next, rwkv_w0, rwkv_w2, rwkv_a0,
           rwkv_a2, rwkv_g2, rwkv_k_k, rwkv_k_a, rwkv_r_k, rwkv_gn_w, rwkv_gn_b, w_out, ln2_g, w_up,
           w_down):
    b, s, dm = x.shape
    depth = w_in.shape[0]
    t = b * s
    head_of = np.arange(WIDTH) // HEAD_DIM
    bd = jnp.asarray(head_of[:, None] == head_of[None, :], BF16)
    out_row = np.arange(PERM_ROWS)
    src_row = DIL_FAR * (out_row % (PERM_ROWS // DIL_FAR)) + out_row // (PERM_ROWS // DIL_FAR)
    perm = jnp.asarray(src_row[:, None] == np.arange(PERM_ROWS)[None, :], BF16)
    near_bias = jnp.asarray(_near_bias_table())

    w_in_b = w_in.astype(BF16)
    w_out_b = w_out.astype(BF16)
    w_up_b = w_up.astype(BF16)
    w_down_b = w_down.astype(BF16)
    vec = lambda p: p.reshape(depth, 1, -1)
    qg = vec(jnp.tile(q_norm_g, (1, N_HEADS)))
    kg = vec(jnp.tile(k_norm_g, (1, N_HEADS)))
    ln1_g, ln2_g, rwkv_gn_w, rwkv_gn_b = vec(ln1_g), vec(ln2_g), vec(rwkv_gn_w), vec(rwkv_gn_b)
    decay_cols = rwkv_w2.shape[2]
    w2_pad = _pad_lora(rwkv_w2, 0)
    w2_hi = w2_pad.astype(BF16)
    prep_params = {
        "tshift_prev": vec(tshift_prev), "tshift_next": vec(tshift_next),
        "w0": rwkv_w0, "w2_hi": w2_hi, "w2_lo": (w2_pad - w2_hi.astype(F32)).astype(BF16),
        "a0": rwkv_a0, "a2": _pad_lora(rwkv_a2, decay_cols).astype(BF16),
        "g2": rwkv_g2.astype(BF16), "k_k": vec(rwkv_k_k), "k_a": vec(rwkv_k_a),
        "r_k": vec(rwkv_r_k),
    }

    x2 = x.reshape(t, dm)
    for layer in range(depth):
        (q, k, va, vb, q16, k16, va16, vb16, r, rv, kap, lw, kt, bb, g, bonus) = _in_proj(
            x2, ln1_g, w_in_b, qg, kg, bd, perm, prep_params, layer, b)
        seq3 = lambda a: a.reshape(b, s, WIDTH)
        both = lambda a: a.reshape(2, b, s, WIDTH)
        flat = lambda a: a.reshape(t, WIDTH)
        far = _attn_far(q16, k16, va16, vb16)
        attn = _attn_near(seq3(q), seq3(k), seq3(va), seq3(vb), far, near_bias)
        yf, yr = _wkv(seq3(r), seq3(rv), seq3(kap), both(lw), both(kt), both(bb))
        x2 = _channel_mix(x2, flat(attn), flat(yf), flat(yr), bonus, g, rwkv_gn_w, rwkv_gn_b, bd,
                          w_out_b, ln2_g, w_up_b, w_down_b, layer)
    return x2.reshape(b, s, dm)
```

```python
import functools
import math

import numpy as np
import jax
import jax.numpy as jnp
from jax import lax
from jax.experimental import pallas as pl
from jax.experimental.pallas import tpu as pltpu

D_MODEL = 1024
HEAD_DIM = 64
N_HEADS = 8
WIDTH = N_HEADS * HEAD_DIM
ATTN_BRANCHES = ((128, 1), (512, 4), (2048, 16))
RADIUS = 64
LORA_BLOCK = 128
GATE_LORA = 128
RWKV_PROJ = 3 * WIDTH + LORA_BLOCK + GATE_LORA
D_FF = 4 * D_MODEL
NORM_EPS = 1e-6
GN_EPS = HEAD_DIM * 1e-5
NEG_INF = -1e30

LANES = 128
ROW_TILE = 512
PERM_ROWS = 256
ATTN_TQ = 128
ATTN_TK = ATTN_TQ + 2 * RADIUS
NEAR_TILES = 4
DIL_FAR = ATTN_BRANCHES[2][1]
NEAR_REACH = RADIUS * ATTN_BRANCHES[1][1]
NEAR_WIN = ATTN_TQ + 2 * NEAR_REACH
NEAR_BIAS_SHIFT = (NEAR_WIN - ATTN_TQ) // ATTN_TQ
LOG2E = 1.4426950408889634
MIX_ROWS = 512
CHUNK = 64
WKV_BATCH = 4
VMEM_LIMIT = 56 * 1024 * 1024

F32 = jnp.float32
BF16 = jnp.bfloat16


def _mm(a, b, prec=None):
    return lax.dot_general(a, b, (((1,), (0,)), ((), ())), precision=prec,
                           preferred_element_type=F32)


def _mm_nt(a, b, prec=None):
    return lax.dot_general(a, b, (((1,), (1,)), ((), ())), precision=prec,
                           preferred_element_type=F32)


def _mm_tn(a, b, prec=None):
    return lax.dot_general(a, b, (((0,), (0,)), ((), ())), precision=prec,
                           preferred_element_type=F32)


def _seg_mean(x, bd):
    return _mm(x.astype(BF16), bd) * (1.0 / HEAD_DIM)


def _rms_rows(x, g):
    return x * lax.rsqrt(jnp.mean(x * x, axis=-1, keepdims=True) + NORM_EPS) * g


def _inproj_kernel(x_ref, xp_ref, xn_ref, g_ref, w_ref, qg_ref, kg_ref, bd_ref, perm_ref, *refs,
                   tiles_per_seq):
    rwkv_params, outs = refs[:10], refs[10:]
    q_ref, k_ref, va_ref, vb_ref, q16_ref, k16_ref, va16_ref, vb16_ref = outs[:8]
    x_all = jnp.concatenate([x_ref[...], xp_ref[...], xn_ref[...]], axis=0)
    h_all = _rms_rows(x_all, g_ref[...]).astype(BF16)
    h = h_all[:MIX_ROWS]
    bd = bd_ref[...]
    u_all = _mm(h_all, w_ref[:, 3 * WIDTH:])
    q = _mm(h, w_ref[:, 0:WIDTH])
    tile = pl.program_id(0) % tiles_per_seq
    halo = xp_ref.shape[0]
    prev_row = jnp.where(tile > 0, u_all[MIX_ROWS + halo - 1:MIX_ROWS + halo], 0.0)
    next_row = jnp.where(tile < tiles_per_seq - 1, u_all[MIX_ROWS + halo:MIX_ROWS + halo + 1], 0.0)
    rwkv = _rwkv_columns(u_all[:MIX_ROWS], prev_row, next_row, bd, *rwkv_params, *outs[8:])
    next(rwkv)
    k = _mm(h, w_ref[:, WIDTH:2 * WIDTH])
    v = _mm(h, w_ref[:, 2 * WIDTH:3 * WIDTH])
    next(rwkv)
    q = q * (lax.rsqrt(_seg_mean(q * q, bd) + NORM_EPS) * qg_ref[...]
             * (HEAD_DIM ** -0.5 * LOG2E))
    k = k * lax.rsqrt(_seg_mean(k * k, bd) + NORM_EPS) * kg_ref[...]
    for _ in rwkv:
        pass
    even_head = (lax.broadcasted_iota(jnp.int32, v.shape, 1) & HEAD_DIM) == 0
    va = jnp.where(even_head, v, 1.0)
    vb = jnp.where(even_head, 1.0, v)
    perm = perm_ref[...]
    for t, nat_ref, res_ref in ((q, q_ref, q16_ref), (k, k_ref, k16_ref), (va, va_ref, va16_ref),
                                (vb, vb_ref, vb16_ref)):
        t = t.astype(BF16)
        nat_ref[...] = t
        per = PERM_ROWS // DIL_FAR
        for part in range(MIX_ROWS // PERM_ROWS):
            grouped = _mm(perm, t[part * PERM_ROWS:(part + 1) * PERM_ROWS]).astype(BF16)
            res_ref[:, part * per:(part + 1) * per, :] = grouped.reshape(DIL_FAR, per, WIDTH)


def _in_proj(x2, ln_g, w_in, qg, kg, bd, perm, p, layer, batch):
    t = x2.shape[0]
    n_in = w_in.shape[-1]
    seq = t // batch
    tiles = seq // MIX_ROWS
    halo = 8
    per_tile = MIX_ROWS // halo
    row = lambda i: (i, 0)
    par = lambda i: (layer, 0, 0)
    par4 = lambda i: (layer, 0, 0, 0)
    res_spec = pl.BlockSpec((None, DIL_FAR, MIX_ROWS // DIL_FAR, WIDTH),
                            lambda i: (i // tiles, 0, i % tiles, 0))
    nat_spec = pl.BlockSpec((MIX_ROWS, WIDTH), row)
    dir_spec = pl.BlockSpec((2, MIX_ROWS, WIDTH), lambda i: (0, i, 0))
    nat = jax.ShapeDtypeStruct((t, WIDTH), BF16)
    res = jax.ShapeDtypeStruct((batch, DIL_FAR, seq // DIL_FAR, WIDTH), BF16)
    both = jax.ShapeDtypeStruct((2, t, WIDTH), BF16)
    return pl.pallas_call(
        functools.partial(_inproj_kernel, tiles_per_seq=tiles),
        grid=(t // MIX_ROWS,),
        in_specs=[
            pl.BlockSpec((MIX_ROWS, D_MODEL), row),
            pl.BlockSpec((halo, D_MODEL), lambda i: (jnp.maximum(i * per_tile - 1, 0), 0)),
            pl.BlockSpec((halo, D_MODEL), lambda i: (jnp.minimum((i + 1) * per_tile, t // halo - 1), 0)),
            pl.BlockSpec((None, 1, D_MODEL), par),
            pl.BlockSpec((None, D_MODEL, n_in), par),
            pl.BlockSpec((None, 1, WIDTH), par),
            pl.BlockSpec((None, 1, WIDTH), par),
            pl.BlockSpec((WIDTH, WIDTH), lambda i: (0, 0)),
            pl.BlockSpec((PERM_ROWS, PERM_ROWS), lambda i: (0, 0)),
            pl.BlockSpec((None, 1, RWKV_PROJ), par),
            pl.BlockSpec((None, 1, RWKV_PROJ), par),
            pl.BlockSpec((None, 2, WIDTH), par),
            pl.BlockSpec((None, 2, LORA_BLOCK, WIDTH), par4),
            pl.BlockSpec((None, 2, WIDTH), par),
            pl.BlockSpec((None, 2, LORA_BLOCK, WIDTH), par4),
            pl.BlockSpec((None, GATE_LORA, WIDTH), par),
            pl.BlockSpec((None, 1, WIDTH), par),
            pl.BlockSpec((None, 1, WIDTH), par),
            pl.BlockSpec((None, 1, WIDTH), par),
        ],
        out_specs=[nat_spec] * 4 + [res_spec] * 4 + [nat_spec] * 3 + [dir_spec] * 3 + [nat_spec] * 2,
        out_shape=([nat] * 4 + [res] * 4 + [nat] * 3
                   + [jax.ShapeDtypeStruct((2, t, WIDTH), F32), both, both] + [nat] * 2),
        compiler_params=pltpu.CompilerParams(dimension_semantics=("parallel",),
                                             vmem_limit_bytes=VMEM_LIMIT),
        name="in_proj",
    )(x2, x2, x2, ln_g, w_in, qg, kg, bd, perm, p["tshift_prev"], p["tshift_next"], p["w0"],
      p["w2"], p["a0"], p["a2"], p["g2"], p["k_k"], p["k_a"], p["r_k"])


def _slope2(h):
    return LOG2E * 2.0 ** (-8.0 * (h + 1) / N_HEADS)


def _softmax_parts(q, kw, va, vb, bias_of_head):
    tq = q.shape[0]
    lane = lax.broadcasted_iota(jnp.int32, (tq, LANES), 1)
    first_head = lane < HEAD_DIM
    heads = range(N_HEADS)
    pair = lambda t, h: t[:, (h // 2) * LANES:(h // 2 + 1) * LANES]
    own = [first_head if h % 2 == 0 else jnp.logical_not(first_head) for h in heads]
    v_own = [pair(va if h % 2 == 0 else vb, h) for h in heads]
    qm = [jnp.where(own[h], pair(q, h), 0.0).astype(BF16) for h in heads]
    s = [_mm_nt(qm[h], pair(kw, h)) + bias_of_head(h) for h in heads]
    m = [jnp.max(s[h], axis=-1, keepdims=True) for h in heads]
    p = [jnp.exp2(s[h] - m[h]).astype(BF16) for h in heads]
    pv = [_mm(p[h], v_own[h]) for h in heads]
    mx = jnp.zeros((tq, LANES), F32)
    for h in heads:
        mx = jnp.where(lane == h, m[h], mx)
    acc = [jnp.where(first_head, pv[h], pv[h + 1]) for h in heads[::2]]
    den = [pltpu.roll(jnp.where(first_head, pv[h + 1], pv[h]), HEAD_DIM, 1) for h in heads[::2]]
    return jnp.concatenate(acc, axis=1), jnp.concatenate(den, axis=1), mx


def _attn_far_kernel(q_ref, k_ref, va_ref, vb_ref, acc_ref, den_ref, mx_ref):
    sub_len = q_ref.shape[0]
    edge = lambda t: jnp.concatenate([t[:RADIUS], t, t[sub_len - RADIUS:]], axis=0)
    kcat, vacat, vbcat = edge(k_ref[...]), edge(va_ref[...]), edge(vb_ref[...])
    rows = lax.broadcasted_iota(jnp.int32, (ATTN_TQ, ATTN_TK), 0)
    cols = lax.broadcasted_iota(jnp.int32, (ATTN_TQ, ATTN_TK), 1)
    dist = jnp.abs(cols - rows - RADIUS)
    for a in range(0, sub_len, ATTN_TQ):
        kidx = a - RADIUS + cols
        valid = (dist <= RADIUS) & (kidx >= 0) & (kidx < sub_len)
        base = jnp.where(valid, dist.astype(F32) * (-float(DIL_FAR)), NEG_INF)
        win = slice(a, a + ATTN_TK)
        acc, den, mx = _softmax_parts(q_ref[a:a + ATTN_TQ, :].astype(F32), kcat[win], vacat[win],
                                      vbcat[win], lambda h: _slope2(h) * base)
        acc_ref[a:a + ATTN_TQ, :] = acc
        den_ref[a:a + ATTN_TQ, :] = den
        mx_ref[a:a + ATTN_TQ, :] = mx


def _attn_far(q16, k16, va16, vb16):
    b, n_res, sub_len, _ = q16.shape
    spec = pl.BlockSpec((None, None, sub_len, WIDTH), lambda bi, r: (bi, r, 0, 0))
    mx_spec = pl.BlockSpec((None, None, sub_len, LANES), lambda bi, r: (bi, r, 0, 0))
    wide = jax.ShapeDtypeStruct((b, n_res, sub_len, WIDTH), F32)
    return pl.pallas_call(
        _attn_far_kernel,
        grid=(b, n_res),
        in_specs=[spec, spec, spec, spec],
        out_specs=[spec, spec, mx_spec],
        out_shape=[wide, wide, jax.ShapeDtypeStruct((b, n_res, sub_len, LANES), F32)],
        compiler_params=pltpu.CompilerParams(dimension_semantics=("parallel", "parallel"),
                                             vmem_limit_bytes=VMEM_LIMIT),
        name="attn_far",
    )(q16, k16, va16, vb16)


def _attn_near_kernel(q_ref, k_ref, va_ref, vb_ref, acc_far_ref, den_far_ref, mx_far_ref, bias_ref,
                      o_ref):
    seq = k_ref.shape[0]
    r1 = lax.broadcasted_iota(jnp.int32, (ATTN_TQ, ATTN_TQ), 0)
    c1 = lax.broadcasted_iota(jnp.int32, (ATTN_TQ, ATTN_TQ), 1)
    per = ATTN_TQ // DIL_FAR
    shift = per.bit_length() - 1
    pos_of = lambda row: DIL_FAR * (row & (per - 1)) + (row >> shift)
    to_state = jnp.where(c1 == pos_of(r1), 1.0, 0.0).astype(BF16)
    to_natural = jnp.where(r1 == pos_of(c1), 1.0, 0.0).astype(BF16)
    er = lax.broadcasted_iota(jnp.int32, (LANES, WIDTH), 0)
    ec = lax.broadcasted_iota(jnp.int32, (LANES, WIDTH), 1)
    spread = jnp.where(er == ec >> (HEAD_DIM.bit_length() - 1), 1.0, 0.0).astype(BF16)

    def per_lane(w):
        hi = w.astype(BF16)
        lo = (w - hi.astype(F32)).astype(BF16)
        return _mm(hi, spread) + _mm(lo, spread)

    for j in range(NEAR_TILES):
        rows = slice(j * ATTN_TQ, (j + 1) * ATTN_TQ)
        far_rows = slice(j * per, (j + 1) * per)
        t0 = (pl.program_id(1) * NEAR_TILES + j) * ATTN_TQ
        w0 = pl.multiple_of(jnp.clip(t0 - NEAR_REACH, 0, seq - NEAR_WIN), ATTN_TQ)
        win = pl.ds(w0, NEAR_WIN)
        q = _mm(to_state, q_ref[rows, :])
        first_blk = (w0 - t0) // ATTN_TQ + NEAR_BIAS_SHIFT
        bias = lambda h: jnp.concatenate(
            [bias_ref[h, first_blk + c] for c in range(NEAR_WIN // ATTN_TQ)], axis=1)
        acc, den, mx = _softmax_parts(q, k_ref[win, :], va_ref[win, :], vb_ref[win, :], bias)

        mx_far = mx_far_ref[:, far_rows, :].reshape(ATTN_TQ, LANES)
        m_max = jnp.maximum(mx, mx_far)
        w_near = per_lane(jnp.exp2(mx - m_max))
        w_far = per_lane(jnp.exp2(mx_far - m_max))
        num = w_near * acc + w_far * acc_far_ref[:, far_rows, :].reshape(ATTN_TQ, WIDTH)
        out = num / (w_near * den + w_far * den_far_ref[:, far_rows, :].reshape(ATTN_TQ, WIDTH))
        o_ref[rows, :] = _mm(to_natural, out.astype(BF16)).astype(o_ref.dtype)


def _near_bias_table():
    per = ATTN_TQ // DIL_FAR
    row = np.arange(ATTN_TQ)
    pos = DIL_FAR * (row % per) + row // per
    n_cols = NEAR_WIN + NEAR_BIAS_SHIFT * ATTN_TQ
    delta = (np.arange(n_cols)[None, :] - NEAR_BIAS_SHIFT * ATTN_TQ) - pos[:, None]
    dist = np.abs(delta)
    count = (dist <= RADIUS * ATTN_BRANCHES[0][1]).astype(np.int32)
    count += (delta % ATTN_BRANCHES[1][1] == 0) & (dist <= NEAR_REACH)
    slopes = np.array([_slope2(h) for h in range(N_HEADS)], np.float64)
    with np.errstate(divide="ignore"):
        tab = -slopes[:, None, None] * dist[None] + np.log2(count)[None]
    tab = np.where(count[None] > 0, tab, NEG_INF).astype(np.float32)
    return tab.reshape(N_HEADS, ATTN_TQ, n_cols // ATTN_TQ, ATTN_TQ).transpose(0, 2, 1, 3)


def _attn_near(q, k, va, vb, far, bias):
    b, s, _ = q.shape
    step_rows = NEAR_TILES * ATTN_TQ
    whole = pl.BlockSpec((None, s, WIDTH), lambda bi, i: (bi, 0, 0))
    blk = lambda width: pl.BlockSpec((None, DIL_FAR, step_rows // DIL_FAR, width),
                                     lambda bi, i: (bi, 0, i, 0))
    rows = pl.BlockSpec((None, step_rows, WIDTH), lambda bi, i: (bi, i, 0))
    return pl.pallas_call(
        _attn_near_kernel,
        grid=(b, s // step_rows),
        in_specs=[rows, whole, whole, whole, blk(WIDTH), blk(WIDTH), blk(LANES),
                  pl.BlockSpec(bias.shape, lambda bi, i: (0, 0, 0, 0))],
        out_specs=rows,
        out_shape=jax.ShapeDtypeStruct((b, s, WIDTH), BF16),
        compiler_params=pltpu.CompilerParams(dimension_semantics=("parallel", "parallel"),
                                             vmem_limit_bytes=VMEM_LIMIT),
        name="attn_near",
    )(q, k, va, vb, *far, bias)


def _sigmoid(x):
    return 1.0 / (1.0 + jnp.exp(-x))


def _rwkv_columns(u, prev_row, next_row, bd, sp_ref, sn_ref, w0_ref, w2_ref, a0_ref, a2_ref,
                  g2_ref, kk_ref, ka_ref, rk_ref,
                  r_ref, v_ref, kap_ref, lw_ref, kt_ref, bb_ref, g_ref, bonus_ref):
    rows = u.shape[0]
    row = lax.broadcasted_iota(jnp.int32, u.shape, 0)
    u_prev = jnp.where(row == 0, prev_row, pltpu.roll(u, 1, 0))
    u_next = jnp.where(row == rows - 1, next_row, pltpu.roll(u, rows - 1, 0))
    um = u + sp_ref[...] * (u_prev - u) + sn_ref[...] * (u_next - u)

    r = um[:, 0:WIDTH]
    k = um[:, WIDTH:2 * WIDTH]
    v = um[:, 2 * WIDTH:3 * WIDTH]
    lora = um[:, 3 * WIDTH:3 * WIDTH + LORA_BLOCK]
    xg = um[:, 3 * WIDTH + LORA_BLOCK:]
    r_ref[...] = r.astype(BF16)
    v_ref[...] = v.astype(BF16)
    g_ref[...] = _mm(_sigmoid(xg).astype(BF16), g2_ref[...]).astype(BF16)
    kk = k * kk_ref[...]
    kap = kk * lax.rsqrt(_seg_mean(kk * kk, bd) * HEAD_DIM + 1e-12)
    kap_ref[...] = kap.astype(BF16)
    tw16 = jnp.tanh(lora).astype(BF16)
    lora16 = lora.astype(BF16)
    kt_sum = jnp.zeros_like(k)
    for d in range(2):
        yield
        z = w0_ref[d:d + 1, :] + _mm(tw16, w2_ref[d])
        lw_ref[d] = _sigmoid(z) * (-math.exp(-0.5))
        a = _sigmoid(a0_ref[d:d + 1, :] + _mm(lora16, a2_ref[d]))
        kt = k * (1.0 + (a - 1.0) * ka_ref[...])
        kt_ref[d] = kt.astype(BF16)
        bb_ref[d] = (a * kap).astype(BF16)
        kt_sum = kt_sum + kt
    bonus_ref[...] = (_seg_mean(r * kt_sum * rk_ref[...], bd) * HEAD_DIM * v).astype(BF16)


def _wkv_kernel(*refs):
    in_refs = (refs[0:6], refs[6:12])
    y_refs = refs[12:14]
    s_ref = refs[14]

    @pl.when(pl.program_id(1) == 0)
    def _():
        s_ref[...] = jnp.zeros_like(s_ref)

    tt = lax.broadcasted_iota(jnp.int32, (CHUNK, LANES), 0)
    lane = lax.broadcasted_iota(jnp.int32, (CHUNK, LANES), 1)
    jj = lane & (HEAD_DIM - 1)
    first_head = lane < HEAD_DIM
    keep_a = jnp.where(first_head, 1.0, 0.0).astype(BF16)
    keep_b = jnp.where(first_head, 0.0, 1.0).astype(BF16)
    rows2 = lax.broadcasted_iota(jnp.int32, (LANES, LANES), 0) < HEAD_DIM
    cols2 = lax.broadcasted_iota(jnp.int32, (LANES, LANES), 1) < HEAD_DIM
    diag_blocks = rows2 == cols2
    eye = (tt == jj).astype(F32)
    same16 = (tt >> 4) == (jj >> 4)
    same32 = (tt >> 5) == (jj >> 5)
    b16 = lambda t: t.astype(BF16)
    bdiag = lambda t: jnp.concatenate([b16(t) * keep_a, b16(t) * keep_b], axis=0)
    mm = lambda x, y: _mm(b16(x), b16(y))
    mm_nt = lambda x, y: _mm_nt(b16(x), b16(y))
    mm_tn = lambda x, y: _mm_tn(b16(x), b16(y))
    pmm = lambda x, y: mm(x, bdiag(y))
    cut = lambda t: [t[:, p * LANES:(p + 1) * LANES] for p in range(N_HEADS // 2)]
    t_small = lax.broadcasted_iota(jnp.int32, (CHUNK, CHUNK), 0)
    j_small = lax.broadcasted_iota(jnp.int32, (CHUNK, CHUNK), 1)

    qt_h, rt_h, v_h, bh_h, kb_h, bbar_h, kh_h, pc_h = ([] for _ in range(8))
    strict, incl, in16, in32, in64 = ([] for _ in range(5))
    n_pair = N_HEADS // 2
    for d, dir_refs in enumerate(in_refs):
        before = (jj < tt) if d == 0 else (jj > tt)
        upto = jnp.logical_or(before, tt == jj)
        tri = jnp.where((j_small <= t_small) if d == 0 else (j_small >= t_small), 1.0, 0.0)
        tri = tri.astype(BF16)
        for bi in range(WKV_BATCH):
            r, v, kap, lw, kt, bb = (ref[bi] for ref in dir_refs)
            r, kap, kt, bb = (t.astype(F32) for t in (r, kap, kt, bb))
            lw_hi = b16(lw)
            rest = lw - lw_hi.astype(F32)
            lw_mid = b16(rest)
            lw_lo = b16(rest - lw_mid.astype(F32))
            cs = _mm(tri, lw_hi) + (_mm(tri, lw_mid) + _mm(tri, lw_lo))
            tot = jnp.sum(lw, axis=0, keepdims=True)
            e_neg = jnp.exp(-cs)
            e_tail = jnp.exp(tot - cs)
            qt_h += cut(kap * jnp.exp(cs - lw))
            rt_h += cut(r * jnp.exp(cs))
            v_h += cut(v)
            kb_h += cut(b16(kt * e_neg))
            bbar_h += cut(b16(bb * e_neg))
            kh_h += cut(b16(kt * e_tail))
            bh_h += cut(b16(bb * e_tail))
            pc_h += cut(jnp.exp(tot))
            strict += [before] * n_pair
            incl += [upto] * n_pair
            in16 += [before & same16] * n_pair
            in32 += [before & same32 & jnp.logical_not(same16)] * n_pair
            in64 += [before & jnp.logical_not(same32)] * n_pair

    each = lambda f, *lists: [f(*args) for args in zip(*lists)]
    unzip = lambda rows: [list(col) for col in zip(*rows)]

    def scores(q, r, k, bb, m_strict, m_incl, m16, m32, m64):
        qr = b16(jnp.concatenate([q, r], axis=0))
        a = mm_nt(qr, jnp.concatenate([bdiag(k), bdiag(bb)], axis=0))
        a_kc = b16(jnp.concatenate([jnp.where(m_strict, a[:CHUNK, :LANES], 0.0),
                                    jnp.where(m_incl, a[CHUNK:, :LANES], 0.0)], axis=0))
        a_qb = a[:CHUNK, LANES:]
        n1 = jnp.where(m16, a_qb, 0.0)
        return (qr[:CHUNK], a_kc, b16(jnp.where(m_incl, a[CHUNK:, LANES:], 0.0)), b16(n1), eye - n1,
                b16(jnp.where(m32, a_qb, 0.0)), b16(jnp.where(m64, a_qb, 0.0)))

    qt16, a_kc, a_rb, n1, t_inv, off32, off64 = unzip(each(
        scores, qt_h, rt_h, kb_h, bbar_h, strict, incl, in16, in32, in64))

    def grow(p, t):
        both = pmm(jnp.concatenate([p, b16(t)], axis=0), p)
        return b16(both[:CHUNK]), t + both[CHUNK:]

    pw = each(lambda n: b16(pmm(n, n)), n1)
    for _ in range(2):
        pw, t_inv = unzip(each(grow, pw, t_inv))
    t_inv = each(lambda t, p: t + pmm(t, p), t_inv, pw)
    for off in (off32, off64):
        t16 = [b16(t) for t in t_inv]
        tmp = each(lambda t, o: b16(pmm(t, o)), t16, off)
        t_inv = each(lambda t, m, t6: t - pmm(m, t6), t_inv, tmp, t16)

    def apply_v(a, vv):
        av = pmm(a, vv)
        return b16(av[:CHUNK]), av[CHUNK:]

    def solve(t, q, a):
        x = mm(t, jnp.concatenate([bdiag(q), bdiag(a)], axis=1))
        return b16(x[:, :LANES]), b16(x[:, LANES:])

    def correct(a, q, u, r, av_r):
        w = mm(a, jnp.concatenate([bdiag(q), bdiag(u)], axis=1))
        return b16(r - w[:, :LANES]), av_r - w[:, LANES:]

    def outer(vv, u, kh, bh):
        t = mm_tn(jnp.concatenate([vv, -u], axis=0), jnp.concatenate([kh, bh], axis=0))
        return jnp.where(first_head, t[:HEAD_DIM], t[HEAD_DIM:])

    av_q, av_r = unzip(each(apply_v, a_kc, v_h))
    xq, xu = unzip(each(solve, t_inv, qt16, av_q))
    rh, y0 = unzip(each(correct, a_rb, xq, xu, rt_h, av_r))
    m1 = each(lambda q, bh: b16(jnp.where(diag_blocks, mm_tn(q, bh), 0.0)), xq, bh_h)
    vk = each(outer, v_h, xu, kh_h, bh_h)

    def advance(i, r, y, m, pc, vk_i):
        st = s_ref[i]
        st16 = b16(st)
        out = mm_nt(r, bdiag(st16)) + y
        s_ref[i] = st * pc - mm(st16, m) + vk_i
        return out

    n_chain = 2 * WKV_BATCH * n_pair
    ys = each(advance, range(n_chain), rh, y0, m1, pc_h, vk)
    for d, y_ref in enumerate(y_refs):
        for bi in range(WKV_BATCH):
            first = (d * WKV_BATCH + bi) * n_pair
            y_ref[bi] = jnp.concatenate(ys[first:first + n_pair], axis=1).astype(y_ref.dtype)


def _wkv(r, v, kap, lw, kt, bb):
    b, s, _ = r.shape
    nc = s // CHUNK
    specs = []
    for d in range(2):
        pos = (lambda c: c) if d == 0 else (lambda c: nc - 1 - c)
        shared = pl.BlockSpec((WKV_BATCH, CHUNK, WIDTH), lambda bi, c, pos=pos: (bi, pos(c), 0))
        direc = pl.BlockSpec((None, WKV_BATCH, CHUNK, WIDTH),
                             lambda bi, c, pos=pos, d=d: (d, bi, pos(c), 0))
        specs.append((shared, direc))
    in_specs = [sp for shared, direc in specs for sp in (shared,) * 3 + (direc,) * 3]
    y_shape = jax.ShapeDtypeStruct((b, s, WIDTH), BF16)
    n_chain = 2 * WKV_BATCH * (N_HEADS // 2)
    return pl.pallas_call(
        _wkv_kernel,
        grid=(b // WKV_BATCH, nc),
        in_specs=in_specs,
        out_specs=[specs[0][0], specs[1][0]],
        out_shape=[y_shape, y_shape],
        scratch_shapes=[pltpu.VMEM((n_chain, HEAD_DIM, LANES), F32)],
        compiler_params=pltpu.CompilerParams(
            dimension_semantics=("parallel", "arbitrary"), vmem_limit_bytes=VMEM_LIMIT),
        name="wkv7",
    )(*((r, v, kap, lw, kt, bb) * 2))


def _channel_kernel(x_ref, attn_ref, yf_ref, yr_ref, bonus_ref, g_ref, gw_ref, gb_ref, bd_ref, w_ref,
                    ln_ref, wu_ref, wd_ref, o_ref):
    bd = bd_ref[...]
    y = yf_ref[...].astype(F32) + yr_ref[...].astype(F32)
    yc = y - _seg_mean(y, bd)
    yn = yc * lax.rsqrt(_seg_mean(yc * yc, bd) + GN_EPS) * gw_ref[...] + gb_ref[...]
    rw = ((yn + bonus_ref[...].astype(F32)) * g_ref[...].astype(F32)).astype(BF16)
    o_ref[...] = x_ref[...] + _mm(attn_ref[...], w_ref[0:WIDTH, :]) + _mm(rw, w_ref[WIDTH:, :])
    x = o_ref[...]
    h = _rms_rows(x, ln_ref[...]).astype(BF16)
    acc = x
    for c in range(0, D_FF, D_MODEL):
        a = jnp.maximum(_mm(h, wu_ref[:, c:c + D_MODEL]), 0.0)
        acc = acc + _mm((a * a).astype(BF16), wd_ref[c:c + D_MODEL, :])
    o_ref[...] = acc


def _channel_mix(x2, attn2, yf2, yr2, bonus2, g2, gn_w, gn_b, bd, w_out, ln_g, w_up, w_down, layer):
    t = x2.shape[0]
    row = lambda i: (i, 0)
    par = lambda i: (layer, 0, 0)
    resident = lambda shape: pl.BlockSpec(shape, par)
    return pl.pallas_call(
        _channel_kernel,
        grid=(t // ROW_TILE,),
        in_specs=[
            pl.BlockSpec((ROW_TILE, D_MODEL), row),
            pl.BlockSpec((ROW_TILE, WIDTH), row),
            pl.BlockSpec((ROW_TILE, WIDTH), row),
            pl.BlockSpec((ROW_TILE, WIDTH), row),
            pl.BlockSpec((ROW_TILE, WIDTH), row),
            pl.BlockSpec((ROW_TILE, WIDTH), row),
            pl.BlockSpec((None, 1, WIDTH), par),
            pl.BlockSpec((None, 1, WIDTH), par),
            pl.BlockSpec((WIDTH, WIDTH), lambda i: (0, 0)),
            resident((None, D_MODEL, D_MODEL)),
            pl.BlockSpec((None, 1, D_MODEL), par),
            resident((None, D_MODEL, D_FF)),
            resident((None, D_FF, D_MODEL)),
        ],
        out_specs=pl.BlockSpec((ROW_TILE, D_MODEL), row),
        out_shape=jax.ShapeDtypeStruct((t, D_MODEL), F32),
        compiler_params=pltpu.CompilerParams(dimension_semantics=("parallel",),
                                             vmem_limit_bytes=VMEM_LIMIT),
        name="channel_mix",
    )(x2, attn2, yf2, yr2, bonus2, g2, gn_w, gn_b, bd, w_out, ln_g, w_up, w_down)


def _pad_lora(w, lo):
    pad = [(0, 0), (0, 0), (lo, LORA_BLOCK - lo - w.shape[2]), (0, 0)]
    return jnp.pad(w, pad)


def kernel(x, ln1_g, w_in, q_norm_g, k_norm_g, tshift_prev, tshift_next, rwkv_w0, rwkv_w2, rwkv_a0,
           rwkv_a2, rwkv_g2, rwkv_k_k, rwkv_k_a, rwkv_r_k, rwkv_gn_w, rwkv_gn_b, w_out, ln2_g, w_up,
           w_down):
    b, s, dm = x.shape
    depth = w_in.shape[0]
    t = b * s
    head_of = np.arange(WIDTH) // HEAD_DIM
    bd = jnp.asarray(head_of[:, None] == head_of[None, :], BF16)
    out_row = np.arange(PERM_ROWS)
    src_row = DIL_FAR * (out_row % (PERM_ROWS // DIL_FAR)) + out_row // (PERM_ROWS // DIL_FAR)
    perm = jnp.asarray(src_row[:, None] == np.arange(PERM_ROWS)[None, :], BF16)
    near_bias = jnp.asarray(_near_bias_table())

    w_in_b = w_in.astype(BF16)
    w_out_b = w_out.astype(BF16)
    w_up_b = w_up.astype(BF16)
    w_down_b = w_down.astype(BF16)
    vec = lambda p: p.reshape(depth, 1, -1)
    qg = vec(jnp.tile(q_norm_g, (1, N_HEADS)))
    kg = vec(jnp.tile(k_norm_g, (1, N_HEADS)))
    ln1_g, ln2_g, rwkv_gn_w, rwkv_gn_b = vec(ln1_g), vec(ln2_g), vec(rwkv_gn_w), vec(rwkv_gn_b)
    decay_cols = rwkv_w2.shape[2]
    prep_params = {
        "tshift_prev": vec(tshift_prev), "tshift_next": vec(tshift_next),
        "w0": rwkv_w0, "w2": _pad_lora(rwkv_w2, 0).astype(BF16),
        "a0": rwkv_a0, "a2": _pad_lora(rwkv_a2, decay_cols).astype(BF16),
        "g2": rwkv_g2.astype(BF16), "k_k": vec(rwkv_k_k), "k_a": vec(rwkv_k_a),
        "r_k": vec(rwkv_r_k),
    }

    x2 = x.reshape(t, dm)
    for layer in range(depth):
        (q, k, va, vb, q16, k16, va16, vb16, r, rv, kap, lw, kt, bb, g, bonus) = _in_proj(
            x2, ln1_g, w_in_b, qg, kg, bd, perm, prep_params, layer, b)
        seq3 = lambda a: a.reshape(b, s, WIDTH)
        both = lambda a: a.reshape(2, b, s, WIDTH)
        flat = lambda a: a.reshape(t, WIDTH)
        far = _attn_far(q16, k16, va16, vb16)
        attn = _attn_near(seq3(q), seq3(k), seq3(va), seq3(vb), far, near_bias)
        yf, yr = _wkv(seq3(r), seq3(rv), seq3(kap), both(lw), both(kt), both(bb))
        x2 = _channel_mix(x2, flat(attn), flat(yf), flat(yr), bonus, g, rwkv_gn_w, rwkv_gn_b, bd,
                          w_out_b, ln2_g, w_up_b, w_down_b, layer)
    return x2.reshape(b, s, dm)
```

```python
import functools
import math

import numpy as np
import jax
import jax.numpy as jnp
from jax import lax
from jax.experimental import pallas as pl
from jax.experimental.pallas import tpu as pltpu

D_MODEL = 1024
HEAD_DIM = 64
N_HEADS = 8
WIDTH = N_HEADS * HEAD_DIM
ATTN_BRANCHES = ((128, 1), (512, 4), (2048, 16))
RADIUS = 64
LORA_BLOCK = 128
GATE_LORA = 128
RWKV_PROJ = 3 * WIDTH + LORA_BLOCK + GATE_LORA
D_FF = 4 * D_MODEL
NORM_EPS = 1e-6
GN_EPS = HEAD_DIM * 1e-5
NEG_INF = -1e30

LANES = 128
ROW_TILE = 512
PERM_ROWS = 256
ATTN_TQ = 128
ATTN_TK = ATTN_TQ + 2 * RADIUS
NEAR_TILES = 4
FAR_GROUP = 4
DIL_FAR = ATTN_BRANCHES[2][1]
NEAR_REACH = RADIUS * ATTN_BRANCHES[1][1]
NEAR_WIN = ATTN_TQ + 2 * NEAR_REACH
NEAR_BIAS_SHIFT = (NEAR_WIN - ATTN_TQ) // ATTN_TQ
LOG2E = 1.4426950408889634
MIX_ROWS = 512
CHUNK = 64
WKV_BATCH = 4
VMEM_LIMIT = 56 * 1024 * 1024

F32 = jnp.float32
BF16 = jnp.bfloat16


def _mm(a, b, prec=None):
    return lax.dot_general(a, b, (((1,), (0,)), ((), ())), precision=prec,
                           preferred_element_type=F32)


def _mm_nt(a, b, prec=None):
    return lax.dot_general(a, b, (((1,), (1,)), ((), ())), precision=prec,
                           preferred_element_type=F32)


def _mm_tn(a, b, prec=None):
    return lax.dot_general(a, b, (((0,), (0,)), ((), ())), precision=prec,
                           preferred_element_type=F32)


def _seg_mean(x, bd):
    return _mm(x.astype(BF16), bd) * (1.0 / HEAD_DIM)


def _rms_rows(x, g):
    return x * lax.rsqrt(jnp.mean(x * x, axis=-1, keepdims=True) + NORM_EPS) * g


def _inproj_kernel(x_ref, xp_ref, xn_ref, g_ref, w_ref, qg_ref, kg_ref, bd_ref, perm_ref, *refs,
                   tiles_per_seq):
    rwkv_params, outs = refs[:10], refs[10:]
    q_ref, k_ref, va_ref, vb_ref, q16_ref, k16_ref, va16_ref, vb16_ref = outs[:8]
    x_all = jnp.concatenate([x_ref[...], xp_ref[...], xn_ref[...]], axis=0)
    h_all = _rms_rows(x_all, g_ref[...]).astype(BF16)
    h = h_all[:MIX_ROWS]
    bd = bd_ref[...]
    u_all = _mm(h_all, w_ref[:, 3 * WIDTH:])
    q = _mm(h, w_ref[:, 0:WIDTH])
    tile = pl.program_id(0) % tiles_per_seq
    halo = xp_ref.shape[0]
    prev_row = jnp.where(tile > 0, u_all[MIX_ROWS + halo - 1:MIX_ROWS + halo], 0.0)
    next_row = jnp.where(tile < tiles_per_seq - 1, u_all[MIX_ROWS + halo:MIX_ROWS + halo + 1], 0.0)
    rwkv = _rwkv_columns(u_all[:MIX_ROWS], prev_row, next_row, bd, *rwkv_params, *outs[8:])
    next(rwkv)
    k = _mm(h, w_ref[:, WIDTH:2 * WIDTH])
    v = _mm(h, w_ref[:, 2 * WIDTH:3 * WIDTH])
    next(rwkv)
    q = q * (lax.rsqrt(_seg_mean(q * q, bd) + NORM_EPS) * qg_ref[...]
             * (HEAD_DIM ** -0.5 * LOG2E))
    k = k * lax.rsqrt(_seg_mean(k * k, bd) + NORM_EPS) * kg_ref[...]
    for _ in rwkv:
        pass
    even_head = (lax.broadcasted_iota(jnp.int32, v.shape, 1) & HEAD_DIM) == 0
    va = jnp.where(even_head, v, 1.0)
    vb = jnp.where(even_head, 1.0, v)
    perm = perm_ref[...]
    for t, nat_ref, res_ref in ((q, q_ref, q16_ref), (k, k_ref, k16_ref), (va, va_ref, va16_ref),
                                (vb, vb_ref, vb16_ref)):
        t = t.astype(BF16)
        nat_ref[...] = t
        per = PERM_ROWS // DIL_FAR
        for part in range(MIX_ROWS // PERM_ROWS):
            grouped = _mm(perm, t[part * PERM_ROWS:(part + 1) * PERM_ROWS]).astype(BF16)
            res_ref[:, part * per:(part + 1) * per, :] = grouped.reshape(DIL_FAR, per, WIDTH)


def _in_proj(x2, ln_g, w_in, qg, kg, bd, perm, p, layer, batch):
    t = x2.shape[0]
    n_in = w_in.shape[-1]
    seq = t // batch
    tiles = seq // MIX_ROWS
    halo = 8
    per_tile = MIX_ROWS // halo
    row = lambda i: (i, 0)
    par = lambda i: (layer, 0, 0)
    par4 = lambda i: (layer, 0, 0, 0)
    res_spec = pl.BlockSpec((None, DIL_FAR, MIX_ROWS // DIL_FAR, WIDTH),
                            lambda i: (i // tiles, 0, i % tiles, 0))
    nat_spec = pl.BlockSpec((MIX_ROWS, WIDTH), row)
    dir_spec = pl.BlockSpec((2, MIX_ROWS, WIDTH), lambda i: (0, i, 0))
    nat = jax.ShapeDtypeStruct((t, WIDTH), BF16)
    res = jax.ShapeDtypeStruct((batch, DIL_FAR, seq // DIL_FAR, WIDTH), BF16)
    both = jax.ShapeDtypeStruct((2, t, WIDTH), BF16)
    return pl.pallas_call(
        functools.partial(_inproj_kernel, tiles_per_seq=tiles),
        grid=(t // MIX_ROWS,),
        in_specs=[
            pl.BlockSpec((MIX_ROWS, D_MODEL), row),
            pl.BlockSpec((halo, D_MODEL), lambda i: (jnp.maximum(i * per_tile - 1, 0), 0)),
            pl.BlockSpec((halo, D_MODEL), lambda i: (jnp.minimum((i + 1) * per_tile, t // halo - 1), 0)),
            pl.BlockSpec((None, 1, D_MODEL), par),
            pl.BlockSpec((None, D_MODEL, n_in), par),
            pl.BlockSpec((None, 1, WIDTH), par),
            pl.BlockSpec((None, 1, WIDTH), par),
            pl.BlockSpec((WIDTH, WIDTH), lambda i: (0, 0)),
            pl.BlockSpec((PERM_ROWS, PERM_ROWS), lambda i: (0, 0)),
            pl.BlockSpec((None, 1, RWKV_PROJ), par),
            pl.BlockSpec((None, 1, RWKV_PROJ), par),
            pl.BlockSpec((None, 2, WIDTH), par),
            pl.BlockSpec((None, 2, LORA_BLOCK, WIDTH), par4),
            pl.BlockSpec((None, 2, WIDTH), par),
            pl.BlockSpec((None, 2, LORA_BLOCK, WIDTH), par4),
            pl.BlockSpec((None, GATE_LORA, WIDTH), par),
            pl.BlockSpec((None, 1, WIDTH), par),
            pl.BlockSpec((None, 1, WIDTH), par),
            pl.BlockSpec((None, 1, WIDTH), par),
        ],
        out_specs=[nat_spec] * 4 + [res_spec] * 4 + [nat_spec] * 3 + [dir_spec] * 3 + [nat_spec] * 2,
        out_shape=([nat] * 4 + [res] * 4 + [nat] * 3
                   + [jax.ShapeDtypeStruct((2, t, WIDTH), F32), both, both] + [nat] * 2),
        compiler_params=pltpu.CompilerParams(dimension_semantics=("parallel",),
                                             vmem_limit_bytes=VMEM_LIMIT),
        name="in_proj",
    )(x2, x2, x2, ln_g, w_in, qg, kg, bd, perm, p["tshift_prev"], p["tshift_next"], p["w0"],
      p["w2"], p["a0"], p["a2"], p["g2"], p["k_k"], p["k_a"], p["r_k"])


def _slope2(h):
    return LOG2E * 2.0 ** (-8.0 * (h + 1) / N_HEADS)


def _softmax_parts(q, kw, va, vb, bias_of_head):
    tq = q.shape[0]
    lane = lax.broadcasted_iota(jnp.int32, (tq, LANES), 1)
    first_head = lane < HEAD_DIM
    heads = range(N_HEADS)
    pair = lambda t, h: t[:, (h // 2) * LANES:(h // 2 + 1) * LANES]
    own = [first_head if h % 2 == 0 else jnp.logical_not(first_head) for h in heads]
    v_own = [pair(va if h % 2 == 0 else vb, h) for h in heads]
    qm = [jnp.where(own[h], pair(q, h), 0.0).astype(BF16) for h in heads]
    s = [_mm_nt(qm[h], pair(kw, h)) + bias_of_head(h) for h in heads]
    m = [jnp.max(s[h], axis=-1, keepdims=True) for h in heads]
    p = [jnp.exp2(s[h] - m[h]).astype(BF16) for h in heads]
    pv = [_mm(p[h], v_own[h]) for h in heads]
    mx = jnp.zeros((tq, LANES), F32)
    for h in heads:
        mx = jnp.where(lane == h, m[h], mx)
    acc = [jnp.where(first_head, pv[h], pv[h + 1]) for h in heads[::2]]
    den = [pltpu.roll(jnp.where(first_head, pv[h + 1], pv[h]), HEAD_DIM, 1) for h in heads[::2]]
    return jnp.concatenate(acc, axis=1), jnp.concatenate(den, axis=1), mx


def _attn_far_kernel(q_ref, k_ref, va_ref, vb_ref, acc_ref, den_ref, mx_ref):
    sub_len = q_ref.shape[1]
    edge = lambda t: jnp.concatenate([t[:RADIUS], t, t[sub_len - RADIUS:]], axis=0)
    rows = lax.broadcasted_iota(jnp.int32, (ATTN_TQ, ATTN_TK), 0)
    cols = lax.broadcasted_iota(jnp.int32, (ATTN_TQ, ATTN_TK), 1)
    dist = jnp.abs(cols - rows - RADIUS)
    for res in range(FAR_GROUP):
        kcat, vacat, vbcat = edge(k_ref[res]), edge(va_ref[res]), edge(vb_ref[res])
        for a in range(0, sub_len, ATTN_TQ):
            kidx = a - RADIUS + cols
            valid = (dist <= RADIUS) & (kidx >= 0) & (kidx < sub_len)
            base = jnp.where(valid, dist.astype(F32) * (-float(DIL_FAR)), NEG_INF)
            win = slice(a, a + ATTN_TK)
            acc, den, mx = _softmax_parts(q_ref[res, a:a + ATTN_TQ, :].astype(F32), kcat[win],
                                          vacat[win], vbcat[win], lambda h: _slope2(h) * base)
            acc_ref[res, a:a + ATTN_TQ, :] = acc
            den_ref[res, a:a + ATTN_TQ, :] = den
            mx_ref[res, a:a + ATTN_TQ, :] = mx


def _attn_far(q16, k16, va16, vb16):
    b, n_res, sub_len, _ = q16.shape
    spec = pl.BlockSpec((None, FAR_GROUP, sub_len, WIDTH), lambda bi, r: (bi, r, 0, 0))
    mx_spec = pl.BlockSpec((None, FAR_GROUP, sub_len, LANES), lambda bi, r: (bi, r, 0, 0))
    wide = jax.ShapeDtypeStruct((b, n_res, sub_len, WIDTH), F32)
    return pl.pallas_call(
        _attn_far_kernel,
        grid=(b, n_res // FAR_GROUP),
        in_specs=[spec, spec, spec, spec],
        out_specs=[spec, spec, mx_spec],
        out_shape=[wide, wide, jax.ShapeDtypeStruct((b, n_res, sub_len, LANES), F32)],
        compiler_params=pltpu.CompilerParams(dimension_semantics=("parallel", "parallel"),
                                             vmem_limit_bytes=VMEM_LIMIT),
        name="attn_far",
    )(q16, k16, va16, vb16)


def _attn_near_kernel(q_ref, k_ref, va_ref, vb_ref, acc_far_ref, den_far_ref, mx_far_ref, bias_ref,
                      o_ref):
    seq = k_ref.shape[0]
    r1 = lax.broadcasted_iota(jnp.int32, (ATTN_TQ, ATTN_TQ), 0)
    c1 = lax.broadcasted_iota(jnp.int32, (ATTN_TQ, ATTN_TQ), 1)
    per = ATTN_TQ // DIL_FAR
    shift = per.bit_length() - 1
    pos_of = lambda row: DIL_FAR * (row & (per - 1)) + (row >> shift)
    to_state = jnp.where(c1 == pos_of(r1), 1.0, 0.0).astype(BF16)
    to_natural = jnp.where(r1 == pos_of(c1), 1.0, 0.0).astype(BF16)
    er = lax.broadcasted_iota(jnp.int32, (LANES, WIDTH), 0)
    ec = lax.broadcasted_iota(jnp.int32, (LANES, WIDTH), 1)
    spread = jnp.where(er == ec >> (HEAD_DIM.bit_length() - 1), 1.0, 0.0).astype(BF16)

    def per_lane(w):
        hi = w.astype(BF16)
        lo = (w - hi.astype(F32)).astype(BF16)
        return _mm(hi, spread) + _mm(lo, spread)

    for j in range(NEAR_TILES):
        rows = slice(j * ATTN_TQ, (j + 1) * ATTN_TQ)
        far_rows = slice(j * per, (j + 1) * per)
        t0 = (pl.program_id(1) * NEAR_TILES + j) * ATTN_TQ
        w0 = pl.multiple_of(jnp.clip(t0 - NEAR_REACH, 0, seq - NEAR_WIN), ATTN_TQ)
        win = pl.ds(w0, NEAR_WIN)
        q = _mm(to_state, q_ref[rows, :])
        first_blk = (w0 - t0) // ATTN_TQ + NEAR_BIAS_SHIFT
        bias = lambda h: jnp.concatenate(
            [bias_ref[h, first_blk + c] for c in range(NEAR_WIN // ATTN_TQ)], axis=1)
        acc, den, mx = _softmax_parts(q, k_ref[win, :], va_ref[win, :], vb_ref[win, :], bias)

        mx_far = mx_far_ref[:, far_rows, :].reshape(ATTN_TQ, LANES)
        m_max = jnp.maximum(mx, mx_far)
        w_near = per_lane(jnp.exp2(mx - m_max))
        w_far = per_lane(jnp.exp2(mx_far - m_max))
        num = w_near * acc + w_far * acc_far_ref[:, far_rows, :].reshape(ATTN_TQ, WIDTH)
        out = num / (w_near * den + w_far * den_far_ref[:, far_rows, :].reshape(ATTN_TQ, WIDTH))
        o_ref[rows, :] = _mm(to_natural, out.astype(BF16)).astype(o_ref.dtype)


def _near_bias_table():
    per = ATTN_TQ // DIL_FAR
    row = np.arange(ATTN_TQ)
    pos = DIL_FAR * (row % per) + row // per
    n_cols = NEAR_WIN + NEAR_BIAS_SHIFT * ATTN_TQ
    delta = (np.arange(n_cols)[None, :] - NEAR_BIAS_SHIFT * ATTN_TQ) - pos[:, None]
    dist = np.abs(delta)
    count = (dist <= RADIUS * ATTN_BRANCHES[0][1]).astype(np.int32)
    count += (delta % ATTN_BRANCHES[1][1] == 0) & (dist <= NEAR_REACH)
    slopes = np.array([_slope2(h) for h in range(N_HEADS)], np.float64)
    with np.errstate(divide="ignore"):
        tab = -slopes[:, None, None] * dist[None] + np.log2(count)[None]
    tab = np.where(count[None] > 0, tab, NEG_INF).astype(np.float32)
    return tab.reshape(N_HEADS, ATTN_TQ, n_cols // ATTN_TQ, ATTN_TQ).transpose(0, 2, 1, 3)


def _attn_near(q, k, va, vb, far, bias):
    b, s, _ = q.shape
    step_rows = NEAR_TILES * ATTN_TQ
    whole = pl.BlockSpec((None, s, WIDTH), lambda bi, i: (bi, 0, 0))
    blk = lambda width: pl.BlockSpec((None, DIL_FAR, step_rows // DIL_FAR, width),
                                     lambda bi, i: (bi, 0, i, 0))
    rows = pl.BlockSpec((None, step_rows, WIDTH), lambda bi, i: (bi, i, 0))
    return pl.pallas_call(
        _attn_near_kernel,
        grid=(b, s // step_rows),
        in_specs=[rows, whole, whole, whole, blk(WIDTH), blk(WIDTH), blk(LANES),
                  pl.BlockSpec(bias.shape, lambda bi, i: (0, 0, 0, 0))],
        out_specs=rows,
        out_shape=jax.ShapeDtypeStruct((b, s, WIDTH), BF16),
        compiler_params=pltpu.CompilerParams(dimension_semantics=("parallel", "parallel"),
                                             vmem_limit_bytes=VMEM_LIMIT),
        name="attn_near",
    )(q, k, va, vb, *far, bias)


def _sigmoid(x):
    return 1.0 / (1.0 + jnp.exp(-x))


def _rwkv_columns(u, prev_row, next_row, bd, sp_ref, sn_ref, w0_ref, w2_ref, a0_ref, a2_ref,
                  g2_ref, kk_ref, ka_ref, rk_ref,
                  r_ref, v_ref, kap_ref, lw_ref, kt_ref, bb_ref, g_ref, bonus_ref):
    rows = u.shape[0]
    row = lax.broadcasted_iota(jnp.int32, u.shape, 0)
    u_prev = jnp.where(row == 0, prev_row, pltpu.roll(u, 1, 0))
    u_next = jnp.where(row == rows - 1, next_row, pltpu.roll(u, rows - 1, 0))
    um = u + sp_ref[...] * (u_prev - u) + sn_ref[...] * (u_next - u)

    r = um[:, 0:WIDTH]
    k = um[:, WIDTH:2 * WIDTH]
    v = um[:, 2 * WIDTH:3 * WIDTH]
    lora = um[:, 3 * WIDTH:3 * WIDTH + LORA_BLOCK]
    xg = um[:, 3 * WIDTH + LORA_BLOCK:]
    r_ref[...] = r.astype(BF16)
    v_ref[...] = v.astype(BF16)
    g_ref[...] = _mm(_sigmoid(xg).astype(BF16), g2_ref[...]).astype(BF16)
    kk = k * kk_ref[...]
    kap = kk * lax.rsqrt(_seg_mean(kk * kk, bd) * HEAD_DIM + 1e-12)
    kap_ref[...] = kap.astype(BF16)
    tw16 = jnp.tanh(lora).astype(BF16)
    lora16 = lora.astype(BF16)
    kt_sum = jnp.zeros_like(k)
    for d in range(2):
        yield
        z = w0_ref[d:d + 1, :] + _mm(tw16, w2_ref[d])
        lw_ref[d] = _sigmoid(z) * (-math.exp(-0.5))
        a = _sigmoid(a0_ref[d:d + 1, :] + _mm(lora16, a2_ref[d]))
        kt = k * (1.0 + (a - 1.0) * ka_ref[...])
        kt_ref[d] = kt.astype(BF16)
        bb_ref[d] = (a * kap).astype(BF16)
        kt_sum = kt_sum + kt
    bonus_ref[...] = (_seg_mean(r * kt_sum * rk_ref[...], bd) * HEAD_DIM * v).astype(BF16)


def _wkv_kernel(*refs):
    in_refs = (refs[0:6], refs[6:12])
    y_refs = refs[12:14]
    s_ref = refs[14]

    @pl.when(pl.program_id(1) == 0)
    def _():
        s_ref[...] = jnp.zeros_like(s_ref)

    tt = lax.broadcasted_iota(jnp.int32, (CHUNK, LANES), 0)
    lane = lax.broadcasted_iota(jnp.int32, (CHUNK, LANES), 1)
    jj = lane & (HEAD_DIM - 1)
    first_head = lane < HEAD_DIM
    keep_a = jnp.where(first_head, 1.0, 0.0).astype(BF16)
    keep_b = jnp.where(first_head, 0.0, 1.0).astype(BF16)
    rows2 = lax.broadcasted_iota(jnp.int32, (LANES, LANES), 0) < HEAD_DIM
    cols2 = lax.broadcasted_iota(jnp.int32, (LANES, LANES), 1) < HEAD_DIM
    diag_blocks = rows2 == cols2
    eye = (tt == jj).astype(F32)
    same16 = (tt >> 4) == (jj >> 4)
    same32 = (tt >> 5) == (jj >> 5)
    b16 = lambda t: t.astype(BF16)
    bdiag = lambda t: jnp.concatenate([b16(t) * keep_a, b16(t) * keep_b], axis=0)
    mm = lambda x, y: _mm(b16(x), b16(y))
    mm_nt = lambda x, y: _mm_nt(b16(x), b16(y))
    mm_tn = lambda x, y: _mm_tn(b16(x), b16(y))
    pmm = lambda x, y: mm(x, bdiag(y))
    cut = lambda t: [t[:, p * LANES:(p + 1) * LANES] for p in range(N_HEADS // 2)]
    t_small = lax.broadcasted_iota(jnp.int32, (CHUNK, CHUNK), 0)
    j_small = lax.broadcasted_iota(jnp.int32, (CHUNK, CHUNK), 1)

    qt_h, rt_h, v_h, bh_h, kb_h, bbar_h, kh_h, pc_h = ([] for _ in range(8))
    strict, incl, in16, in32, in64 = ([] for _ in range(5))
    n_pair = N_HEADS // 2
    for d, dir_refs in enumerate(in_refs):
        before = (jj < tt) if d == 0 else (jj > tt)
        upto = jnp.logical_or(before, tt == jj)
        tri = jnp.where((j_small <= t_small) if d == 0 else (j_small >= t_small), 1.0, 0.0)
        tri = tri.astype(BF16)
        for bi in range(WKV_BATCH):
            r, v, kap, lw, kt, bb = (ref[bi] for ref in dir_refs)
            r, kap, kt, bb = (t.astype(F32) for t in (r, kap, kt, bb))
            lw_hi = b16(lw)
            rest = lw - lw_hi.astype(F32)
            lw_mid = b16(rest)
            lw_lo = b16(rest - lw_mid.astype(F32))
            cs = _mm(tri, lw_hi) + (_mm(tri, lw_mid) + _mm(tri, lw_lo))
            tot = jnp.sum(lw, axis=0, keepdims=True)
            e_neg = jnp.exp(-cs)
            e_tail = jnp.exp(tot - cs)
            qt_h += cut(kap * jnp.exp(cs - lw))
            rt_h += cut(r * jnp.exp(cs))
            v_h += cut(v)
            kb_h += cut(b16(kt * e_neg))
            bbar_h += cut(b16(bb * e_neg))
            kh_h += cut(b16(kt * e_tail))
            bh_h += cut(b16(bb * e_tail))
            pc_h += cut(jnp.exp(tot))
            strict += [before] * n_pair
            incl += [upto] * n_pair
            in16 += [before & same16] * n_pair
            in32 += [before & same32 & jnp.logical_not(same16)] * n_pair
            in64 += [before & jnp.logical_not(same32)] * n_pair

    each = lambda f, *lists: [f(*args) for args in zip(*lists)]
    unzip = lambda rows: [list(col) for col in zip(*rows)]

    def scores(q, r, k, bb, m_strict, m_incl, m16, m32, m64):
        qr = b16(jnp.concatenate([q, r], axis=0))
        a = mm_nt(qr, jnp.concatenate([bdiag(k), bdiag(bb)], axis=0))
        a_kc = b16(jnp.concatenate([jnp.where(m_strict, a[:CHUNK, :LANES], 0.0),
                                    jnp.where(m_incl, a[CHUNK:, :LANES], 0.0)], axis=0))
        a_qb = a[:CHUNK, LANES:]
        n1 = jnp.where(m16, a_qb, 0.0)
        return (qr[:CHUNK], a_kc, b16(jnp.where(m_incl, a[CHUNK:, LANES:], 0.0)), b16(n1), eye - n1,
                b16(jnp.where(m32, a_qb, 0.0)), b16(jnp.where(m64, a_qb, 0.0)))

    qt16, a_kc, a_rb, n1, t_inv, off32, off64 = unzip(each(
        scores, qt_h, rt_h, kb_h, bbar_h, strict, incl, in16, in32, in64))

    def grow(p, t):
        both = pmm(jnp.concatenate([p, b16(t)], axis=0), p)
        return b16(both[:CHUNK]), t + both[CHUNK:]

    pw = each(lambda n: b16(pmm(n, n)), n1)
    for _ in range(2):
        pw, t_inv = unzip(each(grow, pw, t_inv))
    t_inv = each(lambda t, p: t + pmm(t, p), t_inv, pw)
    for off in (off32, off64):
        t16 = [b16(t) for t in t_inv]
        tmp = each(lambda t, o: b16(pmm(t, o)), t16, off)
        t_inv = each(lambda t, m, t6: t - pmm(m, t6), t_inv, tmp, t16)

    def apply_v(a, vv):
        av = pmm(a, vv)
        return b16(av[:CHUNK]), av[CHUNK:]

    def solve(t, q, a):
        x = mm(t, jnp.concatenate([bdiag(q), bdiag(a)], axis=1))
        return b16(x[:, :LANES]), b16(x[:, LANES:])

    def correct(a, q, u, r, av_r):
        w = mm(a, jnp.concatenate([bdiag(q), bdiag(u)], axis=1))
        return b16(r - w[:, :LANES]), av_r - w[:, LANES:]

    def outer(vv, u, kh, bh):
        t = mm_tn(jnp.concatenate([vv, -u], axis=0), jnp.concatenate([kh, bh], axis=0))
        return jnp.where(first_head, t[:HEAD_DIM], t[HEAD_DIM:])

    av_q, av_r = unzip(each(apply_v, a_kc, v_h))
    xq, xu = unzip(each(solve, t_inv, qt16, av_q))
    rh, y0 = unzip(each(correct, a_rb, xq, xu, rt_h, av_r))
    m1 = each(lambda q, bh: b16(jnp.where(diag_blocks, mm_tn(q, bh), 0.0)), xq, bh_h)
    vk = each(outer, v_h, xu, kh_h, bh_h)

    def advance(i, r, y, m, pc, vk_i):
        st = s_ref[i]
        st16 = b16(st)
        out = mm_nt(r, bdiag(st16)) + y
        s_ref[i] = st * pc - mm(st16, m) + vk_i
        return out

    n_chain = 2 * WKV_BATCH * n_pair
    ys = each(advance, range(n_chain), rh, y0, m1, pc_h, vk)
    for d, y_ref in enumerate(y_refs):
        for bi in range(WKV_BATCH):
            first = (d * WKV_BATCH + bi) * n_pair
            y_ref[bi] = jnp.concatenate(ys[first:first + n_pair], axis=1).astype(y_ref.dtype)


def _wkv(r, v, kap, lw, kt, bb):
    b, s, _ = r.shape
    nc = s // CHUNK
    specs = []
    for d in range(2):
        pos = (lambda c: c) if d == 0 else (lambda c: nc - 1 - c)
        shared = pl.BlockSpec((WKV_BATCH, CHUNK, WIDTH), lambda bi, c, pos=pos: (bi, pos(c), 0))
        direc = pl.BlockSpec((None, WKV_BATCH, CHUNK, WIDTH),
                             lambda bi, c, pos=pos, d=d: (d, bi, pos(c), 0))
        specs.append((shared, direc))
    in_specs = [sp for shared, direc in specs for sp in (shared,) * 3 + (direc,) * 3]
    y_shape = jax.ShapeDtypeStruct((b, s, WIDTH), BF16)
    n_chain = 2 * WKV_BATCH * (N_HEADS // 2)
    return pl.pallas_call(
        _wkv_kernel,
        grid=(b // WKV_BATCH, nc),
        in_specs=in_specs,
        out_specs=[specs[0][0], specs[1][0]],
        out_shape=[y_shape, y_shape],
        scratch_shapes=[pltpu.VMEM((n_chain, HEAD_DIM, LANES), F32)],
        compiler_params=pltpu.CompilerParams(
            dimension_semantics=("parallel", "arbitrary"), vmem_limit_bytes=VMEM_LIMIT),
        name="wkv7",
    )(*((r, v, kap, lw, kt, bb) * 2))


def _channel_kernel(x_ref, attn_ref, yf_ref, yr_ref, bonus_ref, g_ref, gw_ref, gb_ref, bd_ref, w_ref,
                    ln_ref, wu_ref, wd_ref, o_ref):
    bd = bd_ref[...]
    y = yf_ref[...].astype(F32) + yr_ref[...].astype(F32)
    yc = y - _seg_mean(y, bd)
    yn = yc * lax.rsqrt(_seg_mean(yc * yc, bd) + GN_EPS) * gw_ref[...] + gb_ref[...]
    rw = ((yn + bonus_ref[...].astype(F32)) * g_ref[...].astype(F32)).astype(BF16)
    o_ref[...] = x_ref[...] + _mm(attn_ref[...], w_ref[0:WIDTH, :]) + _mm(rw, w_ref[WIDTH:, :])
    x = o_ref[...]
    h = _rms_rows(x, ln_ref[...]).astype(BF16)
    acc = x
    for c in range(0, D_FF, D_MODEL):
        a = jnp.maximum(_mm(h, wu_ref[:, c:c + D_MODEL]), 0.0)
        acc = acc + _mm((a * a).astype(BF16), wd_ref[c:c + D_MODEL, :])
    o_ref[...] = acc


def _channel_mix(x2, attn2, yf2, yr2, bonus2, g2, gn_w, gn_b, bd, w_out, ln_g, w_up, w_down, layer):
    t = x2.shape[0]
    row = lambda i: (i, 0)
    par = lambda i: (layer, 0, 0)
    resident = lambda shape: pl.BlockSpec(shape, par)
    return pl.pallas_call(
        _channel_kernel,
        grid=(t // ROW_TILE,),
        in_specs=[
            pl.BlockSpec((ROW_TILE, D_MODEL), row),
            pl.BlockSpec((ROW_TILE, WIDTH), row),
            pl.BlockSpec((ROW_TILE, WIDTH), row),
            pl.BlockSpec((ROW_TILE, WIDTH), row),
            pl.BlockSpec((ROW_TILE, WIDTH), row),
            pl.BlockSpec((ROW_TILE, WIDTH), row),
            pl.BlockSpec((None, 1, WIDTH), par),
            pl.BlockSpec((None, 1, WIDTH), par),
            pl.BlockSpec((WIDTH, WIDTH), lambda i: (0, 0)),
            resident((None, D_MODEL, D_MODEL)),
            pl.BlockSpec((None, 1, D_MODEL), par),
            resident((None, D_MODEL, D_FF)),
            resident((None, D_FF, D_MODEL)),
        ],
        out_specs=pl.BlockSpec((ROW_TILE, D_MODEL), row),
        out_shape=jax.ShapeDtypeStruct((t, D_MODEL), F32),
        compiler_params=pltpu.CompilerParams(dimension_semantics=("parallel",),
                                             vmem_limit_bytes=VMEM_LIMIT),
        name="channel_mix",
    )(x2, attn2, yf2, yr2, bonus2, g2, gn_w, gn_b, bd, w_out, ln_g, w_up, w_down)


def _pad_lora(w, lo):
    pad = [(0, 0), (0, 0), (lo, LORA_BLOCK - lo - w.shape[2]), (0, 0)]
    return jnp.pad(w, pad)


def kernel(x, ln1_g, w_in, q_norm_g, k_norm_g, tshift_prev, tshift_next, rwkv_w0, rwkv_w2, rwkv_a0,
           rwkv_a2, rwkv_g2, rwkv_k_k, rwkv_k_a, rwkv_r_k, rwkv_gn_w, rwkv_gn_b, w_out, ln2_g, w_up,
           w_down):
    b, s, dm = x.shape
    depth = w_in.shape[0]
    t = b * s
    head_of = np.arange(WIDTH) // HEAD_DIM
    bd = jnp.asarray(head_of[:, None] == head_of[None, :], BF16)
    out_row = np.arange(PERM_ROWS)
    src_row = DIL_FAR * (out_row % (PERM_ROWS // DIL_FAR)) + out_row // (PERM_ROWS // DIL_FAR)
    perm = jnp.asarray(src_row[:, None] == np.arange(PERM_ROWS)[None, :], BF16)
    near_bias = jnp.asarray(_near_bias_table())

    w_in_b = w_in.astype(BF16)
    w_out_b = w_out.astype(BF16)
    w_up_b = w_up.astype(BF16)
    w_down_b = w_down.astype(BF16)
    vec = lambda p: p.reshape(depth, 1, -1)
    qg = vec(jnp.tile(q_norm_g, (1, N_HEADS)))
    kg = vec(jnp.tile(k_norm_g, (1, N_HEADS)))
    ln1_g, ln2_g, rwkv_gn_w, rwkv_gn_b = vec(ln1_g), vec(ln2_g), vec(rwkv_gn_w), vec(rwkv_gn_b)
    decay_cols = rwkv_w2.shape[2]
    prep_params = {
        "tshift_prev": vec(tshift_prev), "tshift_next": vec(tshift_next),
        "w0": rwkv_w0, "w2": _pad_lora(rwkv_w2, 0).astype(BF16),
        "a0": rwkv_a0, "a2": _pad_lora(rwkv_a2, decay_cols).astype(BF16),
        "g2": rwkv_g2.astype(BF16), "k_k": vec(rwkv_k_k), "k_a": vec(rwkv_k_a),
        "r_k": vec(rwkv_r_k),
    }

    x2 = x.reshape(t, dm)
    for layer in range(depth):
        (q, k, va, vb, q16, k16, va16, vb16, r, rv, kap, lw, kt, bb, g, bonus) = _in_proj(
            x2, ln1_g, w_in_b, qg, kg, bd, perm, prep_params, layer, b)
        seq3 = lambda a: a.reshape(b, s, WIDTH)
        both = lambda a: a.reshape(2, b, s, WIDTH)
        flat = lambda a: a.reshape(t, WIDTH)
        far = _attn_far(q16, k16, va16, vb16)
        attn = _attn_near(seq3(q), seq3(k), seq3(va), seq3(vb), far, near_bias)
        yf, yr = _wkv(seq3(r), seq3(rv), seq3(kap), both(lw), both(kt), both(bb))
        x2 = _channel_mix(x2, flat(attn), flat(yf), flat(yr), bonus, g, rwkv_gn_w, rwkv_gn_b, bd,
                          w_out_b, ln2_g, w_up_b, w_down_b, layer)
    return x2.reshape(b, s, dm)
```
